```python
import jax, jax.numpy as jnp
from jax import lax
import numpy as np

D_MODEL = 1024
BATCH = 16
SEQ = 4096
DEPTH = 4
DEC_BATCH = 8
DEC_SEQ = 32
PAST_LEN = 4096

CHUNK = 64
N_MEM = 256
EPS = 1e-6
GDN_HEADS = 4
GDN_DK = 128
GDN_DV = 128
CONV_W = 4
GDN_QK = GDN_HEADS * GDN_DK
GDN_V = GDN_HEADS * GDN_DV
GDN_CONV_CH = 2 * GDN_QK + GDN_V
GLA_HEADS = 4
GLA_DK = 64
GLA_DV = 128
GLA_RANK = 16
GLA_TAU = 16.0
GLA_QK = GLA_HEADS * GLA_DK
GLA_V = GLA_HEADS * GLA_DV
MIX_W = GDN_V + GLA_V
XA_HEADS = 4
XA_DH = D_MODEL // XA_HEADS
D_FF = 2816
IN_SIZES = (GDN_CONV_CH, GDN_HEADS, GDN_HEADS, GDN_V, GLA_QK, GLA_QK, GLA_V, GLA_V, GLA_RANK)
IN_COLS = sum(IN_SIZES)
IN_OFFSETS = tuple(int(o) for o in np.cumsum(IN_SIZES)[:-1])

kernel_name = "hymba_gdn_gla_macaron_streaming_step"


def rmsnorm(x, g):
    xf = x.astype(jnp.float32)
    y = xf * lax.rsqrt(jnp.mean(xf * xf, axis=-1, keepdims=True) + EPS)
    return (y * g.astype(jnp.float32)).astype(x.dtype)


def l2norm(x):
    return x * lax.rsqrt(jnp.sum(x * x, axis=-1, keepdims=True) + EPS)


def swiglu(h, wg, wu, wd):
    return (jax.nn.silu(h @ wg) * (h @ wu)) @ wd


def causal_conv(u, buf, w):
    t = u.shape[1]
    full = jnp.concatenate([buf.astype(u.dtype), u], axis=1)
    out = full[:, 0:t] * w[0]
    for i in range(1, CONV_W):
        out = out + full[:, i:i + t] * w[i]
    return out, full[:, -(CONV_W - 1):]


def to_chunks(a, n, c):
    b, _, h = a.shape[:3]
    a = a.reshape((b, n, c, h) + a.shape[3:])
    a = jnp.swapaxes(a, 2, 3)
    return jnp.moveaxis(a, 1, 0)


def from_chunks(o):
    n, b, h, c, d = o.shape
    return jnp.transpose(o, (1, 0, 3, 2, 4)).reshape(b, n * c, h, d)


def gated_delta_chunked(q, k, v, g, beta, s0):
    t = q.shape[1]
    c = min(t, CHUNK)
    n = t // c
    qc, kc, vc = to_chunks(q, n, c), to_chunks(k, n, c), to_chunks(v, n, c)
    gc, bc = to_chunks(g, n, c), to_chunks(beta, n, c)
    G = jnp.cumsum(gc, axis=-1)
    incl = jnp.tril(jnp.ones((c, c), dtype=bool))
    strict = jnp.tril(jnp.ones((c, c), dtype=bool), -1)
    diff = G[..., :, None] - G[..., None, :]
    decay = jnp.where(incl, jnp.exp(jnp.where(incl, diff, 0.0)), 0.0)
    kb = kc * bc[..., None]
    A = jnp.where(strict, jnp.einsum('nbhid,nbhjd->nbhij', kb, kc) * decay, 0.0)
    IA = A + jnp.eye(c, dtype=A.dtype)
    u = lax.linalg.triangular_solve(IA, vc * bc[..., None], left_side=True, lower=True, unit_diagonal=True)
    w = lax.linalg.triangular_solve(IA, kb * jnp.exp(G)[..., None], left_side=True, lower=True, unit_diagonal=True)

    def step(S, inp):
        qi, ki, ui, wi, Gi, Di = inp
        unew = ui - jnp.einsum('bhld,bhdv->bhlv', wi, S)
        o = (jnp.einsum('bhld,bhdv->bhlv', qi * jnp.exp(Gi)[..., None], S)
             + jnp.einsum('bhij,bhjv->bhiv', jnp.einsum('bhid,bhjd->bhij', qi, ki) * Di, unew))
        glast = Gi[..., -1]
        S = (S * jnp.exp(glast)[..., None, None]
             + jnp.einsum('bhld,bhlv->bhdv', ki * jnp.exp(glast[..., None] - Gi)[..., None], unew))
        return S, o

    s_new, o = lax.scan(step, s0, (qc, kc, u, w, G, decay))
    return from_chunks(o), s_new


def gla_chunked(q, k, v, log_a, s0):
    t = q.shape[1]
    c = min(t, CHUNK)
    n = t // c
    qc, kc, vc = to_chunks(q, n, c), to_chunks(k, n, c), to_chunks(v, n, c)
    Bc = jnp.cumsum(to_chunks(log_a, n, c), axis=3)
    incl = jnp.tril(jnp.ones((c, c), dtype=bool))[:, :, None]

    def step(S, inp):
        qi, ki, vi, Bi = inp
        diff = Bi[:, :, :, None, :] - Bi[:, :, None, :, :]
        dec = jnp.exp(jnp.where(incl, diff, -jnp.inf))
        P = jnp.einsum('bhid,bhjd,bhijd->bhij', qi, ki, dec)
        o = jnp.einsum('bhid,bhdv->bhiv', qi * jnp.exp(Bi), S) + jnp.einsum('bhij,bhjv->bhiv', P, vi)
        Bl = Bi[:, :, -1:, :]
        S = (S * jnp.exp(Bl[:, :, 0, :])[..., None]
             + jnp.einsum('bhjd,bhjv->bhdv', ki * jnp.exp(Bl - Bi), vi))
        return S, o

    s_new, o = lax.scan(step, s0, (qc, kc, vc, Bc))
    return from_chunks(o), s_new


def token_mix(h, conv_buf, s_gdn, s_gla, W, l):
    b, t, _ = h.shape
    f32 = jnp.float32
    p = h @ W['w_in'][l]
    qkv, b_raw, a_raw, z, gq, gk, gv, gr, glr = jnp.split(p, IN_OFFSETS, axis=-1)
    qkv, new_buf = causal_conv(qkv, conv_buf, W['gdn_conv_w'][l])
    qkv = jax.nn.silu(qkv).astype(f32)
    qa, ka, va = jnp.split(qkv, [GDN_QK, 2 * GDN_QK], axis=-1)
    qa = l2norm(qa.reshape(b, t, GDN_HEADS, GDN_DK)) * (GDN_DK ** -0.5)
    ka = l2norm(ka.reshape(b, t, GDN_HEADS, GDN_DK))
    va = va.reshape(b, t, GDN_HEADS, GDN_DV)
    beta = jax.nn.sigmoid(b_raw.astype(f32))
    g = -jnp.exp(W['gdn_a_log'][l].astype(f32)) * jax.nn.softplus(a_raw.astype(f32) + W['gdn_dt_bias'][l].astype(f32))
    oa, s_gdn_new = gated_delta_chunked(qa, ka, va, g, beta, s_gdn.astype(f32))
    oa = rmsnorm(oa, W['gdn_out_norm'][l]) * jax.nn.silu(z.astype(f32).reshape(b, t, GDN_HEADS, GDN_DV))
    qb = gq.astype(f32).reshape(b, t, GLA_HEADS, GLA_DK) * (GLA_DK ** -0.5)
    kb = gk.astype(f32).reshape(b, t, GLA_HEADS, GLA_DK)
    vb = gv.astype(f32).reshape(b, t, GLA_HEADS, GLA_DV)
    log_a = jax.nn.log_sigmoid((glr @ W['gla_gate_up'][l] + W['gla_gate_bias'][l]).astype(f32)) / GLA_TAU
    ob, s_gla_new = gla_chunked(qb, kb, vb, log_a.reshape(b, t, GLA_HEADS, GLA_DK), s_gla.astype(f32))
    ob = rmsnorm(ob, W['gla_out_norm'][l]) * jax.nn.silu(gr.astype(f32).reshape(b, t, GLA_HEADS, GLA_DV))
    o = jnp.concatenate([oa.reshape(b, t, GDN_V), ob.reshape(b, t, GLA_V)], axis=-1).astype(h.dtype)
    return o @ W['w_out'][l], new_buf, s_gdn_new, s_gla_new


def cross_attn(h, mk, mv, wq, wo):
    b, t, _ = h.shape
    q = (h @ wq).reshape(b, t, XA_HEADS, XA_DH)
    s = jnp.einsum('bthd,bmhd->bhtm', q, mk.astype(h.dtype)).astype(jnp.float32) * (XA_DH ** -0.5)
    pr = jax.nn.softmax(s, axis=-1).astype(h.dtype)
    o = jnp.einsum('bhtm,bmhd->bthd', pr, mv.astype(h.dtype)).reshape(b, t, D_MODEL)
    return o @ wo


def trunk(x, mem_k, mem_v, conv_state, gdn_state, gla_state, W):
    convs, gdns, glas = [], [], []
    for l in range(DEPTH):
        h = rmsnorm(x, W['ffn1_norm'][l])
        x = x + 0.5 * swiglu(h, W['ffn1_w_gate'][l], W['ffn1_w_up'][l], W['ffn1_w_down'][l])
        h = rmsnorm(x, W['mix_norm'][l])
        o, cb, sa, sb = token_mix(h, conv_state[l], gdn_state[l], gla_state[l], W, l)
        x = x + o
        h = rmsnorm(x, W['xattn_norm'][l])
        x = x + cross_attn(h, mem_k[l], mem_v[l], W['xattn_w_q'][l], W['xattn_w_o'][l])
        h = rmsnorm(x, W['ffn2_norm'][l])
        x = x + 0.5 * swiglu(h, W['ffn2_w_gate'][l], W['ffn2_w_up'][l], W['ffn2_w_down'][l])
        convs.append(cb)
        gdns.append(sa)
        glas.append(sb)
    y = rmsnorm(x, W['final_norm'])
    return y, jnp.stack(convs), jnp.stack(gdns), jnp.stack(glas)


def setup_inputs(seed: int = 0) -> dict:
    key = jax.random.key(seed)
    ks = iter(jax.random.split(key, 48))
    f32 = jnp.float32

    def nrm(shape, scale):
        return jax.random.normal(next(ks), shape, f32) * scale

    def gain(shape):
        return 1.0 + nrm(shape, 0.02)

    dt = jnp.exp(jax.random.uniform(next(ks), (DEPTH, GDN_HEADS), f32, minval=float(np.log(1e-3)), maxval=float(np.log(1e-1))))
    return {
        'x_prompt': nrm((BATCH, SEQ, D_MODEL), 1.0),
        'x_sample': nrm((DEC_BATCH, DEC_SEQ, D_MODEL), 1.0),
        'mem_prompt': nrm((BATCH, N_MEM, D_MODEL), 1.0),
        'cache_mem_k': nrm((DEPTH, DEC_BATCH, N_MEM, XA_HEADS, XA_DH), 1.0),
        'cache_mem_v': nrm((DEPTH, DEC_BATCH, N_MEM, XA_HEADS, XA_DH), 1.0),
        'state_gdn_conv': nrm((DEPTH, DEC_BATCH, CONV_W - 1, GDN_CONV_CH), 1.0),
        'state_gdn': nrm((DEPTH, DEC_BATCH, GDN_HEADS, GDN_DK, GDN_DV), 0.1),
        'state_gla': nrm((DEPTH, DEC_BATCH, GLA_HEADS, GLA_DK, GLA_DV), 0.5),
        'ffn1_norm': gain((DEPTH, D_MODEL)),
        'ffn1_w_gate': nrm((DEPTH, D_MODEL, D_FF), D_MODEL ** -0.5),
        'ffn1_w_up': nrm((DEPTH, D_MODEL, D_FF), D_MODEL ** -0.5),
        'ffn1_w_down': nrm((DEPTH, D_FF, D_MODEL), D_FF ** -0.5),
        'mix_norm': gain((DEPTH, D_MODEL)),
        'w_in': nrm((DEPTH, D_MODEL, IN_COLS), D_MODEL ** -0.5),
        'gdn_conv_w': nrm((DEPTH, CONV_W, GDN_CONV_CH), CONV_W ** -0.5),
        'gdn_a_log': jnp.log(jax.random.uniform(next(ks), (DEPTH, GDN_HEADS), f32, minval=1.0, maxval=16.0)),
        'gdn_dt_bias': dt + jnp.log(-jnp.expm1(-dt)),
        'gdn_out_norm': gain((DEPTH, GDN_DV)),
        'gla_gate_up': nrm((DEPTH, GLA_RANK, GLA_QK), GLA_RANK ** -0.5),
        'gla_gate_bias': nrm((DEPTH, GLA_QK), 0.1),
        'gla_out_norm': gain((DEPTH, GLA_DV)),
        'w_out': nrm((DEPTH, MIX_W, D_MODEL), MIX_W ** -0.5),
        'xattn_norm': gain((DEPTH, D_MODEL)),
        'xattn_w_q': nrm((DEPTH, D_MODEL, D_MODEL), D_MODEL ** -0.5),
        'xattn_w_k': nrm((DEPTH, D_MODEL, D_MODEL), D_MODEL ** -0.5),
        'xattn_w_v': nrm((DEPTH, D_MODEL, D_MODEL), D_MODEL ** -0.5),
        'xattn_w_o': nrm((DEPTH, D_MODEL, D_MODEL), D_MODEL ** -0.5),
        'ffn2_norm': gain((DEPTH, D_MODEL)),
        'ffn2_w_gate': nrm((DEPTH, D_MODEL, D_FF), D_MODEL ** -0.5),
        'ffn2_w_up': nrm((DEPTH, D_MODEL, D_FF), D_MODEL ** -0.5),
        'ffn2_w_down': nrm((DEPTH, D_FF, D_MODEL), D_FF ** -0.5),
        'final_norm': gain((D_MODEL,)),
    }


def reference(x_prompt, x_sample, mem_prompt, cache_mem_k, cache_mem_v, state_gdn_conv, state_gdn, state_gla,
              ffn1_norm, ffn1_w_gate, ffn1_w_up, ffn1_w_down, mix_norm, w_in, gdn_conv_w, gdn_a_log, gdn_dt_bias,
              gdn_out_norm, gla_gate_up, gla_gate_bias, gla_out_norm, w_out, xattn_norm, xattn_w_q, xattn_w_k,
              xattn_w_v, xattn_w_o, ffn2_norm, ffn2_w_gate, ffn2_w_up, ffn2_w_down, final_norm):
    W = dict(ffn1_norm=ffn1_norm, ffn1_w_gate=ffn1_w_gate, ffn1_w_up=ffn1_w_up, ffn1_w_down=ffn1_w_down,
             mix_norm=mix_norm, w_in=w_in, gdn_conv_w=gdn_conv_w, gdn_a_log=gdn_a_log, gdn_dt_bias=gdn_dt_bias,
             gdn_out_norm=gdn_out_norm, gla_gate_up=gla_gate_up, gla_gate_bias=gla_gate_bias,
             gla_out_norm=gla_out_norm, w_out=w_out, xattn_norm=xattn_norm, xattn_w_q=xattn_w_q,
             xattn_w_o=xattn_w_o, ffn2_norm=ffn2_norm, ffn2_w_gate=ffn2_w_gate, ffn2_w_up=ffn2_w_up,
             ffn2_w_down=ffn2_w_down, final_norm=final_norm)
    bp = x_prompt.shape[0]
    mem_k_p = jnp.einsum('bmd,lde->lbme', mem_prompt, xattn_w_k).reshape(DEPTH, bp, N_MEM, XA_HEADS, XA_DH)
    mem_v_p = jnp.einsum('bmd,lde->lbme', mem_prompt, xattn_w_v).reshape(DEPTH, bp, N_MEM, XA_HEADS, XA_DH)
    conv0 = jnp.zeros((DEPTH, bp, CONV_W - 1, GDN_CONV_CH), x_prompt.dtype)
    gdn0 = jnp.zeros((DEPTH, bp, GDN_HEADS, GDN_DK, GDN_DV), jnp.float32)
    gla0 = jnp.zeros((DEPTH, bp, GLA_HEADS, GLA_DK, GLA_DV), jnp.float32)
    y_prompt, conv_p, gdn_p, gla_p = trunk(x_prompt, mem_k_p, mem_v_p, conv0, gdn0, gla0, W)
    y_sample, conv_s, gdn_s, gla_s = trunk(x_sample, cache_mem_k, cache_mem_v, state_gdn_conv, state_gdn, state_gla, W)
    return (y_prompt, y_sample, conv_p, gdn_p, gla_p, mem_k_p, mem_v_p, conv_s, gdn_s, gla_s)
```

```python
import functools
import math

import numpy as np
import jax
import jax.numpy as jnp
from jax import lax
from jax.experimental import pallas as pl
from jax.experimental.pallas import tpu as pltpu

F32 = jnp.float32
BF16 = jnp.bfloat16

D_MODEL = 1024
DEPTH = 4
EPS = 1e-6
N_MEM = 256
CONV_W = 4
GDN_HEADS, GDN_DK, GDN_DV = 4, 128, 128
GDN_QK = GDN_HEADS * GDN_DK
GDN_V = GDN_HEADS * GDN_DV
GDN_CONV_CH = 2 * GDN_QK + GDN_V
GLA_HEADS, GLA_DK, GLA_DV = 4, 64, 128
GLA_RANK = 16
GLA_TAU = 16.0
GLA_QK = GLA_HEADS * GLA_DK
GLA_V = GLA_HEADS * GLA_DV
MIX_W = GDN_V + GLA_V
XA_HEADS = 4
XA_DH = D_MODEL // XA_HEADS
D_FF = 2816
IN_SIZES = (GDN_CONV_CH, GDN_HEADS, GDN_HEADS, GDN_V, GLA_QK, GLA_QK, GLA_V, GLA_V, GLA_RANK)
IN_OFFSETS = tuple(int(o) for o in np.cumsum((0,) + IN_SIZES)[:-1])

LANES = 128
SUBLANES = 8
C_QKV = 0
C_Z = C_QKV + GDN_CONV_CH
C_GQ = C_Z + GDN_V
C_GK = C_GQ + GLA_QK
C_GV = C_GK + GLA_QK
C_GR = C_GV + GLA_V
C_SM = C_GR + GLA_V
IN_COLS_PAD = C_SM + LANES
SM_B, SM_A, SM_R = 0, GDN_HEADS, 2 * GDN_HEADS

VMEM_LIMIT = 56 * 1024 * 1024
MIX_TILE = 256
ROW_TILE = 512
FF_CHUNK = 256


def _dot(a, b):
    return jnp.dot(a, b, preferred_element_type=F32)


def _dot_nt(a, b):
    return lax.dot_general(a, b, (((1,), (1,)), ((), ())), preferred_element_type=F32)


def _dot_tn(a, b):
    return lax.dot_general(a, b, (((0,), (0,)), ((), ())), preferred_element_type=F32)


def _rms(x, g):
    return x * lax.rsqrt(jnp.mean(x * x, axis=-1, keepdims=True) + EPS) * g


def _silu(x):
    return x * jax.nn.sigmoid(x)


def _softplus(x):
    return jnp.maximum(x, 0.0) + jnp.log1p(jnp.exp(-jnp.abs(x)))


def _split2(x):
    hi = x.astype(BF16)
    lo = (x - hi.astype(F32)).astype(BF16)
    return hi, lo


def _ffn_body(x_ref, g_ref, wg_ref, wu_ref, wd_ref, fg_ref, o_ref, a_ref, *, final):
    x = x_ref[...]
    h = _rms(x, g_ref[...]).astype(BF16)
    for f in range(0, D_FF, FF_CHUNK):
        gt = _dot(h, wg_ref[:, f:f + FF_CHUNK])
        up = _dot(h, wu_ref[:, f:f + FF_CHUNK])
        a_ref[:, f:f + FF_CHUNK] = (_silu(gt) * up).astype(BF16)
    y = x + 0.5 * _dot(a_ref[...], wd_ref[...])
    if final:
        y = _rms(y, fg_ref[...])
    o_ref[...] = y


def _ffn(x2, l, gain, wg, wu, wd, fgain, final):
    m = x2.shape[0]
    tm = min(m, ROW_TILE)
    wspec = lambda shp: pl.BlockSpec((None,) + shp, lambda i: (l, 0, 0))
    return pl.pallas_call(
        functools.partial(_ffn_body, final=final),
        grid=(m // tm,),
        in_specs=[
            pl.BlockSpec((tm, D_MODEL), lambda i: (i, 0)),
            wspec((1, D_MODEL)),
            wspec((D_MODEL, D_FF)),
            wspec((D_MODEL, D_FF)),
            wspec((D_FF, D_MODEL)),
            pl.BlockSpec((1, D_MODEL), lambda i: (0, 0)),
        ],
        out_specs=pl.BlockSpec((tm, D_MODEL), lambda i: (i, 0)),
        out_shape=jax.ShapeDtypeStruct((m, D_MODEL), F32),
        scratch_shapes=[pltpu.VMEM((tm, D_FF), BF16)],
        compiler_params=pltpu.CompilerParams(
            dimension_semantics=("arbitrary",), vmem_limit_bytes=VMEM_LIMIT),
        name="ffn",
    )(x2, gain, wg, wu, wd, fgain)


def _memkv_body(m_ref, wk_ref, wv_ref, k_ref, v_ref):
    m = m_ref[...]
    k_ref[...] = _dot(m, wk_ref[...])
    v_ref[...] = _dot(m, wv_ref[...])


def _memkv(mem2, wk, wv):
    m = mem2.shape[0]
    tm = min(m, 1024)
    return pl.pallas_call(
        _memkv_body,
        grid=(DEPTH, m // tm),
        in_specs=[
            pl.BlockSpec((tm, D_MODEL), lambda l, i: (i, 0)),
            pl.BlockSpec((None, D_MODEL, D_MODEL), lambda l, i: (l, 0, 0)),
            pl.BlockSpec((None, D_MODEL, D_MODEL), lambda l, i: (l, 0, 0)),
        ],
        out_specs=[
            pl.BlockSpec((None, tm, D_MODEL), lambda l, i: (l, i, 0)),
            pl.BlockSpec((None, tm, D_MODEL), lambda l, i: (l, i, 0)),
        ],
        out_shape=[jax.ShapeDtypeStruct((DEPTH, m, D_MODEL), F32)] * 2,
        compiler_params=pltpu.CompilerParams(
            dimension_semantics=("arbitrary", "arbitrary"), vmem_limit_bytes=VMEM_LIMIT),
        name="memkv",
    )(mem2, wk, wv)


def _xattn_body(x_ref, g_ref, wq_ref, wo_ref, mk_ref, mv_ref, o_ref):
    x = x_ref[...]
    h = _rms(x, g_ref[...]).astype(BF16)
    q = _dot(h, wq_ref[...])
    mk = mk_ref[...].astype(BF16)
    mv = mv_ref[...].astype(BF16)
    outs = []
    for hh in range(XA_HEADS):
        sl = slice(hh * XA_DH, (hh + 1) * XA_DH)
        s = _dot_nt(q[:, sl].astype(BF16), mk[:, sl]) * (XA_DH ** -0.5)
        s = s - jnp.max(s, axis=-1, keepdims=True)
        e = jnp.exp(s)
        pr = e / jnp.sum(e, axis=-1, keepdims=True)
        outs.append(_dot(pr.astype(BF16), mv[:, sl]))
    o = jnp.concatenate(outs, axis=1).astype(BF16)
    o_ref[...] = x + _dot(o, wo_ref[...])


def _xattn(x, l, gain, wq, wo, mk, mv):
    b, t, _ = x.shape
    tt = min(t, ROW_TILE)
    wspec = lambda shp: pl.BlockSpec((None,) + shp, lambda i, j: (l, 0, 0))
    return pl.pallas_call(
        _xattn_body,
        grid=(b, t // tt),
        in_specs=[
            pl.BlockSpec((None, tt, D_MODEL), lambda i, j: (i, j, 0)),
            wspec((1, D_MODEL)),
            wspec((D_MODEL, D_MODEL)),
            wspec((D_MODEL, D_MODEL)),
            pl.BlockSpec((None, None, N_MEM, D_MODEL), lambda i, j: (l, i, 0, 0)),
            pl.BlockSpec((None, None, N_MEM, D_MODEL), lambda i, j: (l, i, 0, 0)),
        ],
        out_specs=pl.BlockSpec((None, tt, D_MODEL), lambda i, j: (i, j, 0)),
        out_shape=jax.ShapeDtypeStruct(x.shape, F32),
        compiler_params=pltpu.CompilerParams(
            dimension_semantics=("arbitrary", "arbitrary"), vmem_limit_bytes=VMEM_LIMIT),
        name="xattn",
    )(x, gain, wq, wo, mk, mv)


def _level_consts(c):
    nlev = int(math.log2(c))
    assert 1 << nlev == c
    i = np.arange(c)[:, None]
    j = np.arange(c)[None, :]
    mats = []
    for l in list(range(1, nlev)) + [nlev]:
        same = (i >> l) == (j >> l)
        mats.append(same & (j <= i))
        mats.append(same & (j > i))
    return jnp.asarray(np.stack(mats).astype(np.float32), dtype=BF16), nlev


def _mix_body(x_ref, g_ref, win_ref, wst_ref, cw_ref, vec_ref, rowp_ref, gup_ref, gb_ref, gng_ref, gnl_ref,
              wout_ref, cs_ref, conv0_ref, sg0_ref, sl0_ref,
              y_ref, convo_ref, sgo_ref, slo_ref,
              cbuf, sgdn, sgla, *, tt, nlev):
    t = pl.program_id(1)
    nt = pl.num_programs(1)
    tail = SUBLANES - (CONV_W - 1)

    @pl.when(t == 0)
    def _():
        cbuf[tail:SUBLANES, :] = conv0_ref[...]
        sgdn[...] = sg0_ref[...]
        sgla[...] = sl0_ref[...]

    x = x_ref[...]
    h = _rms(x, g_ref[...]).astype(BF16)
    p = _dot(h, win_ref[...])

    cbuf[SUBLANES:SUBLANES + tt, :] = p[:, C_QKV:C_QKV + GDN_CONV_CH]
    cw = cw_ref[...]
    conv = cbuf[tail:tail + tt, :] * cw[0:1, :]
    for i in range(1, CONV_W):
        conv = conv + cbuf[tail + i:tail + i + tt, :] * cw[i:i + 1, :]
    new_tail = cbuf[tail + tt:SUBLANES + tt, :]
    cbuf[tail:SUBLANES, :] = new_tail
    qkv = _silu(conv)

    ii = lax.broadcasted_iota(jnp.int32, (tt, tt), 0)
    jj = lax.broadcasted_iota(jnp.int32, (tt, tt), 1)
    lower = ii > jj
    lv = jnp.where(lower, 31 - lax.clz(ii ^ jj), -1)
    eye = jnp.where(ii == jj, 1.0, 0.0).astype(F32)
    lfull = cs_ref[2 * (nlev - 1)]

    sm = p[:, C_SM:C_SM + LANES]
    vec = vec_ref[...]
    beta_c = jax.nn.sigmoid(sm)
    g_c = vec[0:1, :] * _softplus(sm + vec[1:2, :])
    g_hi, g_lo = _split2(g_c)
    gcum_c = _dot(lfull, g_hi) + _dot(lfull, g_lo)
    st = _dot_nt(wst_ref[...], h)
    rowp = rowp_ref[...]
    g_r = rowp[0] * _softplus(st + rowp[1])
    r_hi, r_lo = _split2(g_r)
    gcum_r = _dot_nt(r_hi, lfull) + _dot_nt(r_lo, lfull)

    outs = []
    for hd in range(GDN_HEADS):
        qh = qkv[:, hd * GDN_DK:(hd + 1) * GDN_DK]
        kh = qkv[:, GDN_QK + hd * GDN_DK:GDN_QK + (hd + 1) * GDN_DK]
        vh = qkv[:, 2 * GDN_QK + hd * GDN_DV:2 * GDN_QK + (hd + 1) * GDN_DV]
        qh = qh * lax.rsqrt(jnp.sum(qh * qh, axis=-1, keepdims=True) + EPS) * (GDN_DK ** -0.5)
        kh = kh * lax.rsqrt(jnp.sum(kh * kh, axis=-1, keepdims=True) + EPS)
        bcol = beta_c[:, SM_B + hd:SM_B + hd + 1]
        gc = gcum_c[:, SM_A + hd:SM_A + hd + 1]
        gr = gcum_r[SM_A + hd:SM_A + hd + 1, :]
        dec = jnp.exp(jnp.minimum(gc - gr, 0.0))
        kb = kh * bcol
        kk = _dot_nt(jnp.concatenate([kb, qh], axis=0).astype(BF16), kh.astype(BF16))
        a = jnp.where(lower, kk[:tt] * dec, 0.0)
        qk = jnp.where(ii >= jj, kk[tt:] * dec, 0.0)
        tinv = eye - jnp.where(lv == 0, a, 0.0)
        for l in range(1, nlev):
            aoff = jnp.where(lv == l, a, 0.0).astype(BF16)
            tb = tinv.astype(BF16)
            tinv = tinv - _dot(tb, _dot(aoff, tb).astype(BF16))
        eg = jnp.exp(gc)
        rhs = jnp.concatenate([vh * bcol, kb * eg], axis=1).astype(BF16)
        uw = _dot(tinv.astype(BF16), rhs)
        s_old = sgdn[hd]
        wq = _dot(jnp.concatenate([uw[:, GDN_DV:], qh * eg], axis=0).astype(BF16), s_old.astype(BF16))
        unew = (uw[:, :GDN_DV] - wq[:tt]).astype(BF16)
        o = wq[tt:] + _dot(qk.astype(BF16), unew)
        glast = gr[:, tt - 1:tt]
        kg = (kh * jnp.exp(glast - gc)).astype(BF16)
        sgdn[hd] = s_old * jnp.exp(glast) + _dot_tn(kg, unew)
        on = _rms(o, gng_ref[...])
        outs.append(on * _silu(p[:, C_Z + hd * GDN_DV:C_Z + (hd + 1) * GDN_DV]))

    xg = _dot(sm.astype(BF16), gup_ref[...]) + gb_ref[...]
    la = -_softplus(-xg) * (1.0 / GLA_TAU)
    la_hi, la_lo = _split2(la)

    def relsum(idx):
        m = cs_ref[idx]
        return _dot(m, la_hi) + _dot(m, la_lo)

    lane_head = lax.broadcasted_iota(jnp.int32, (1, GLA_QK), 1) // GLA_DK

    def stack_heads(v):
        return jnp.concatenate(
            [jnp.where(lane_head == hd, v, 0.0) for hd in range(GLA_HEADS)], axis=0).astype(BF16)

    q2 = p[:, C_GQ:C_GQ + GLA_QK] * (GLA_DK ** -0.5)
    k2 = p[:, C_GK:C_GK + GLA_QK]
    v2 = p[:, C_GV:C_GV + GLA_V].astype(BF16)
    ii4 = jnp.concatenate([ii] * GLA_HEADS, axis=0)
    jj4 = jnp.concatenate([jj] * GLA_HEADS, axis=0)
    lv4 = jnp.concatenate([lv] * GLA_HEADS, axis=0)
    pm = jnp.where(ii4 == jj4, _dot_nt(stack_heads(q2), k2.astype(BF16)), 0.0)
    for l in range(nlev):
        if l == 0:
            qt, kt = q2 * jnp.exp(la), k2
        else:
            qt = q2 * jnp.exp(relsum(2 * (l - 1)))
            kt = k2 * jnp.exp(relsum(2 * (l - 1) + 1))
        pm = jnp.where(lv4 == l, _dot_nt(stack_heads(qt), kt.astype(BF16)), pm)
    bcum = relsum(2 * (nlev - 1))
    brev = relsum(2 * (nlev - 1) + 1)
    s2 = sgla[...]
    inter = _dot(stack_heads(q2 * jnp.exp(bcum)), s2.astype(BF16))
    ds = _dot_tn((k2 * jnp.exp(brev)).astype(BF16), v2)
    ones = jnp.ones((tt, GLA_DV), BF16)
    blast = _dot_tn(la_hi, ones) + _dot_tn(la_lo, ones)
    for hd in range(GLA_HEADS):
        vs = slice(hd * GLA_DV, (hd + 1) * GLA_DV)
        rs = slice(hd * GLA_DK, (hd + 1) * GLA_DK)
        o = _dot(pm[hd * tt:(hd + 1) * tt].astype(BF16), v2[:, vs]) + inter[hd * tt:(hd + 1) * tt]
        sgla[rs, :] = s2[rs, :] * jnp.exp(blast[rs, :]) + ds[rs, vs]
        on = _rms(o, gnl_ref[...])
        outs.append(on * _silu(p[:, C_GR + hd * GLA_DV:C_GR + (hd + 1) * GLA_DV]))

    o_all = jnp.concatenate(outs, axis=1).astype(BF16)
    y_ref[...] = x + _dot(o_all, wout_ref[...])

    @pl.when(t == nt - 1)
    def _():
        convo_ref[...] = new_tail
        sgo_ref[...] = sgdn[...]
        slo_ref[...] = sgla[...]


def _mix(x, l, W, conv0, sg0, sl0):
    b, t, _ = x.shape
    tt = min(t, MIX_TILE)
    cs, nlev = _level_consts(tt)
    ncs = cs.shape[0]
    rowp = jnp.broadcast_to(W['rowp'][l][:, :, None], (2, 2 * SUBLANES, tt))
    wspec = lambda shp: pl.BlockSpec((None,) + shp, lambda i, j: (l,) + (0,) * len(shp))
    full = lambda shp: pl.BlockSpec(shp, lambda i, j: (0,) * len(shp))
    sspec = lambda shp: pl.BlockSpec((None,) + shp, lambda i, j: (i,) + (0,) * len(shp))
    return pl.pallas_call(
        functools.partial(_mix_body, tt=tt, nlev=nlev),
        grid=(b, t // tt),
        in_specs=[
            pl.BlockSpec((None, tt, D_MODEL), lambda i, j: (i, j, 0)),
            wspec((1, D_MODEL)),
            wspec((D_MODEL, IN_COLS_PAD)),
            wspec((2 * SUBLANES, D_MODEL)),
            wspec((CONV_W, GDN_CONV_CH)),
            wspec((SUBLANES, LANES)),
            full((2, 2 * SUBLANES, tt)),
            wspec((LANES, GLA_QK)),
            wspec((1, GLA_QK)),
            wspec((1, GDN_DV)),
            wspec((1, GLA_DV)),
            wspec((MIX_W, D_MODEL)),
            full((ncs, tt, tt)),
            sspec((CONV_W - 1, GDN_CONV_CH)),
            sspec((GDN_HEADS, GDN_DK, GDN_DV)),
            sspec((GLA_QK, GLA_DV)),
        ],
        out_specs=[
            pl.BlockSpec((None, tt, D_MODEL), lambda i, j: (i, j, 0)),
            sspec((CONV_W - 1, GDN_CONV_CH)),
            sspec((GDN_HEADS, GDN_DK, GDN_DV)),
            sspec((GLA_QK, GLA_DV)),
        ],
        out_shape=[
            jax.ShapeDtypeStruct(x.shape, F32),
            jax.ShapeDtypeStruct((b, CONV_W - 1, GDN_CONV_CH), F32),
            jax.ShapeDtypeStruct((b, GDN_HEADS, GDN_DK, GDN_DV), F32),
            jax.ShapeDtypeStruct((b, GLA_QK, GLA_DV), F32),
        ],
        scratch_shapes=[
            pltpu.VMEM((SUBLANES + tt, GDN_CONV_CH), F32),
            pltpu.VMEM((GDN_HEADS, GDN_DK, GDN_DV), F32),
            pltpu.VMEM((GLA_QK, GLA_DV), F32),
        ],
        compiler_params=pltpu.CompilerParams(
            dimension_semantics=("arbitrary", "arbitrary"), vmem_limit_bytes=VMEM_LIMIT),
        name="mix",
    )(x, W['mix_norm'], W['w_in'], W['w_st'], W['gdn_conv_w'], W['vec'], rowp, W['gup'], W['gla_gate_bias'],
      W['gdn_out_norm'], W['gla_out_norm'], W['w_out'], cs, conv0, sg0, sl0)


def _prep_weights(ffn1_norm, ffn1_w_gate, ffn1_w_up, ffn1_w_down, mix_norm, w_in, gdn_conv_w, gdn_a_log,
                  gdn_dt_bias, gdn_out_norm, gla_gate_up, gla_gate_bias, gla_out_norm, w_out, xattn_norm,
                  xattn_w_q, xattn_w_o, ffn2_norm, ffn2_w_gate, ffn2_w_up, ffn2_w_down, final_norm):
    o = IN_OFFSETS
    seg = lambda k: w_in[:, :, o[k]:o[k] + IN_SIZES[k]]
    small = jnp.concatenate([seg(1), seg(2), seg(8)], axis=-1)
    small = jnp.pad(small, ((0, 0), (0, 0), (0, LANES - small.shape[-1])))
    w_in_r = jnp.concatenate([seg(0), seg(3), seg(4), seg(5), seg(6), seg(7), small], axis=-1).astype(BF16)
    w_st = jnp.concatenate([seg(1), seg(2)], axis=-1)
    w_st = jnp.pad(jnp.swapaxes(w_st, 1, 2), ((0, 0), (0, SUBLANES), (0, 0))).astype(BF16)
    nega = -jnp.exp(gdn_a_log.astype(F32))
    dtb = gdn_dt_bias.astype(F32)
    lane_pad = lambda v: jnp.pad(v, ((0, 0), (SM_A, LANES - SM_A - GDN_HEADS)))
    vec = jnp.stack([lane_pad(nega), lane_pad(dtb)], axis=1)
    vec = jnp.pad(vec, ((0, 0), (0, SUBLANES - 2), (0, 0)))
    row_pad = lambda v: jnp.pad(v, ((0, 0), (SM_A, 2 * SUBLANES - SM_A - GDN_HEADS)))
    rowp = jnp.stack([row_pad(nega), row_pad(dtb)], axis=1)
    gup = jnp.pad(gla_gate_up, ((0, 0), (SM_R, LANES - SM_R - GLA_RANK), (0, 0))).astype(BF16)
    r3 = lambda v: v[:, None, :].astype(F32)
    return dict(
        ffn1_norm=r3(ffn1_norm), ffn1_w_gate=ffn1_w_gate.astype(BF16), ffn1_w_up=ffn1_w_up.astype(BF16),
        ffn1_w_down=ffn1_w_down.astype(BF16), mix_norm=r3(mix_norm), w_in=w_in_r, w_st=w_st,
        gdn_conv_w=gdn_conv_w.astype(F32), vec=vec, rowp=rowp, gup=gup, gla_gate_bias=r3(gla_gate_bias),
        gdn_out_norm=r3(gdn_out_norm), gla_out_norm=r3(gla_out_norm), w_out=w_out.astype(BF16),
        xattn_norm=r3(xattn_norm), xattn_w_q=xattn_w_q.astype(BF16), xattn_w_o=xattn_w_o.astype(BF16),
        ffn2_norm=r3(ffn2_norm), ffn2_w_gate=ffn2_w_gate.astype(BF16), ffn2_w_up=ffn2_w_up.astype(BF16),
        ffn2_w_down=ffn2_w_down.astype(BF16), final_norm=final_norm[None, :].astype(F32))


def _trunk(x, mem_k, mem_v, conv_state, gdn_state, gla_state, W):
    b, t, _ = x.shape
    mk = mem_k.reshape(DEPTH, b, N_MEM, D_MODEL)
    mv = mem_v.reshape(DEPTH, b, N_MEM, D_MODEL)
    gla_state = gla_state.reshape(DEPTH, b, GLA_QK, GLA_DV)
    convs, gdns, glas = [], [], []
    for l in range(DEPTH):
        x2 = _ffn(x.reshape(b * t, D_MODEL), l, W['ffn1_norm'], W['ffn1_w_gate'], W['ffn1_w_up'],
                  W['ffn1_w_down'], W['final_norm'], False)
        x, cb, sa, sb = _mix(x2.reshape(b, t, D_MODEL), l, W, conv_state[l], gdn_state[l], gla_state[l])
        x = _xattn(x, l, W['xattn_norm'], W['xattn_w_q'], W['xattn_w_o'], mk, mv)
        x2 = _ffn(x.reshape(b * t, D_MODEL), l, W['ffn2_norm'], W['ffn2_w_gate'], W['ffn2_w_up'],
                  W['ffn2_w_down'], W['final_norm'], l == DEPTH - 1)
        x = x2.reshape(b, t, D_MODEL)
        convs.append(cb)
        gdns.append(sa)
        glas.append(sb.reshape(b, GLA_HEADS, GLA_DK, GLA_DV))
    return x, jnp.stack(convs), jnp.stack(gdns), jnp.stack(glas)


def kernel(x_prompt, x_sample, mem_prompt, cache_mem_k, cache_mem_v, state_gdn_conv, state_gdn, state_gla, ffn1_norm, ffn1_w_gate, ffn1_w_up, ffn1_w_down, mix_norm, w_in, gdn_conv_w, gdn_a_log, gdn_dt_bias, gdn_out_norm, gla_gate_up, gla_gate_bias, gla_out_norm, w_out, xattn_norm, xattn_w_q, xattn_w_k, xattn_w_v, xattn_w_o, ffn2_norm, ffn2_w_gate, ffn2_w_up, ffn2_w_down, final_norm):
    W = _prep_weights(ffn1_norm, ffn1_w_gate, ffn1_w_up, ffn1_w_down, mix_norm, w_in, gdn_conv_w, gdn_a_log,
                      gdn_dt_bias, gdn_out_norm, gla_gate_up, gla_gate_bias, gla_out_norm, w_out, xattn_norm,
                      xattn_w_q, xattn_w_o, ffn2_norm, ffn2_w_gate, ffn2_w_up, ffn2_w_down, final_norm)
    bp = x_prompt.shape[0]
    mem2 = mem_prompt.reshape(bp * N_MEM, D_MODEL).astype(BF16)
    mk_p, mv_p = _memkv(mem2, xattn_w_k.astype(BF16), xattn_w_v.astype(BF16))
    mem_k_p = mk_p.reshape(DEPTH, bp, N_MEM, XA_HEADS, XA_DH)
    mem_v_p = mv_p.reshape(DEPTH, bp, N_MEM, XA_HEADS, XA_DH)
    conv0 = jnp.zeros((DEPTH, bp, CONV_W - 1, GDN_CONV_CH), F32)
    gdn0 = jnp.zeros((DEPTH, bp, GDN_HEADS, GDN_DK, GDN_DV), F32)
    gla0 = jnp.zeros((DEPTH, bp, GLA_HEADS, GLA_DK, GLA_DV), F32)
    y_p, conv_p, gdn_p, gla_p = _trunk(x_prompt, mem_k_p, mem_v_p, conv0, gdn0, gla0, W)
    y_s, conv_s, gdn_s, gla_s = _trunk(x_sample, cache_mem_k, cache_mem_v, state_gdn_conv, state_gdn,
                                       state_gla, W)
    return (y_p, y_s, conv_p, gdn_p, gla_p, mem_k_p, mem_v_p, conv_s, gdn_s, gla_s)
```

```python
import functools
import math

import numpy as np
import jax
import jax.numpy as jnp
from jax import lax
from jax.experimental import pallas as pl
from jax.experimental.pallas import tpu as pltpu

F32 = jnp.float32
BF16 = jnp.bfloat16

D_MODEL = 1024
DEPTH = 4
EPS = 1e-6
N_MEM = 256
CONV_W = 4
GDN_HEADS, GDN_DK, GDN_DV = 4, 128, 128
GDN_QK = GDN_HEADS * GDN_DK
GDN_V = GDN_HEADS * GDN_DV
GDN_CONV_CH = 2 * GDN_QK + GDN_V
GLA_HEADS, GLA_DK, GLA_DV = 4, 64, 128
GLA_RANK = 16
GLA_TAU = 16.0
GLA_QK = GLA_HEADS * GLA_DK
GLA_V = GLA_HEADS * GLA_DV
MIX_W = GDN_V + GLA_V
XA_HEADS = 4
XA_DH = D_MODEL // XA_HEADS
D_FF = 2816
IN_SIZES = (GDN_CONV_CH, GDN_HEADS, GDN_HEADS, GDN_V, GLA_QK, GLA_QK, GLA_V, GLA_V, GLA_RANK)
IN_OFFSETS = tuple(int(o) for o in np.cumsum((0,) + IN_SIZES)[:-1])

LANES = 128
SUBLANES = 8
C_QKV = 0
C_Z = C_QKV + GDN_CONV_CH
C_GQ = C_Z + GDN_V
C_GK = C_GQ + GLA_QK
C_GV = C_GK + GLA_QK
C_GR = C_GV + GLA_V
C_SM = C_GR + GLA_V
IN_COLS_PAD = C_SM + LANES
SM_B, SM_A, SM_R = 0, GDN_HEADS, 2 * GDN_HEADS

VMEM_LIMIT = 56 * 1024 * 1024
MIX_TILE = 256
ROW_TILE = 512
FF_CHUNK = 256


def _dot(a, b):
    return jnp.dot(a, b, preferred_element_type=F32)


def _dot_nt(a, b):
    return lax.dot_general(a, b, (((1,), (1,)), ((), ())), preferred_element_type=F32)


def _dot_tn(a, b):
    return lax.dot_general(a, b, (((0,), (0,)), ((), ())), preferred_element_type=F32)


def _rms(x, g):
    return x * lax.rsqrt(jnp.mean(x * x, axis=-1, keepdims=True) + EPS) * g


def _silu(x):
    return x * jax.nn.sigmoid(x)


def _softplus(x):
    return jnp.maximum(x, 0.0) + jnp.log(1.0 + jnp.exp(-jnp.abs(x)))


def _split2(x):
    hi = x.astype(BF16)
    lo = (x - hi.astype(F32)).astype(BF16)
    return hi, lo


def _ffn_body(x_ref, g_ref, wg_ref, wu_ref, wd_ref, fg_ref, o_ref, a_ref, *, final):
    x = x_ref[...]
    h = _rms(x, g_ref[...]).astype(BF16)
    for f in range(0, D_FF, FF_CHUNK):
        gt = _dot(h, wg_ref[:, f:f + FF_CHUNK])
        up = _dot(h, wu_ref[:, f:f + FF_CHUNK])
        a_ref[:, f:f + FF_CHUNK] = (_silu(gt) * up).astype(BF16)
    y = x + 0.5 * _dot(a_ref[...], wd_ref[...])
    if final:
        y = _rms(y, fg_ref[...])
    o_ref[...] = y


def _ffn(x2, l, gain, wg, wu, wd, fgain, final):
    m = x2.shape[0]
    tm = min(m, ROW_TILE)
    wspec = lambda shp: pl.BlockSpec((None,) + shp, lambda i: (l, 0, 0))
    return pl.pallas_call(
        functools.partial(_ffn_body, final=final),
        grid=(m // tm,),
        in_specs=[
            pl.BlockSpec((tm, D_MODEL), lambda i: (i, 0)),
            wspec((1, D_MODEL)),
            wspec((D_MODEL, D_FF)),
            wspec((D_MODEL, D_FF)),
            wspec((D_FF, D_MODEL)),
            pl.BlockSpec((1, D_MODEL), lambda i: (0, 0)),
        ],
        out_specs=pl.BlockSpec((tm, D_MODEL), lambda i: (i, 0)),
        out_shape=jax.ShapeDtypeStruct((m, D_MODEL), F32),
        scratch_shapes=[pltpu.VMEM((tm, D_FF), BF16)],
        compiler_params=pltpu.CompilerParams(
            dimension_semantics=("arbitrary",), vmem_limit_bytes=VMEM_LIMIT),
        name="ffn",
    )(x2, gain, wg, wu, wd, fgain)


def _memkv_body(m_ref, wk_ref, wv_ref, k_ref, v_ref):
    m = m_ref[...]
    k_ref[...] = _dot(m, wk_ref[...])
    v_ref[...] = _dot(m, wv_ref[...])


def _memkv(mem2, wk, wv):
    m = mem2.shape[0]
    tm = min(m, 1024)
    return pl.pallas_call(
        _memkv_body,
        grid=(DEPTH, m // tm),
        in_specs=[
            pl.BlockSpec((tm, D_MODEL), lambda l, i: (i, 0)),
            pl.BlockSpec((None, D_MODEL, D_MODEL), lambda l, i: (l, 0, 0)),
            pl.BlockSpec((None, D_MODEL, D_MODEL), lambda l, i: (l, 0, 0)),
        ],
        out_specs=[
            pl.BlockSpec((None, tm, D_MODEL), lambda l, i: (l, i, 0)),
            pl.BlockSpec((None, tm, D_MODEL), lambda l, i: (l, i, 0)),
        ],
        out_shape=[jax.ShapeDtypeStruct((DEPTH, m, D_MODEL), F32)] * 2,
        compiler_params=pltpu.CompilerParams(
            dimension_semantics=("arbitrary", "arbitrary"), vmem_limit_bytes=VMEM_LIMIT),
        name="memkv",
    )(mem2, wk, wv)


def _xattn_body(x_ref, g_ref, wq_ref, wo_ref, mk_ref, mv_ref, o_ref):
    x = x_ref[...]
    h = _rms(x, g_ref[...]).astype(BF16)
    q = _dot(h, wq_ref[...])
    mk = mk_ref[...].astype(BF16)
    mv = mv_ref[...].astype(BF16)
    sls = [slice(hh * XA_DH, (hh + 1) * XA_DH) for hh in range(XA_HEADS)]
    ss = [_dot_nt(q[:, sl].astype(BF16), mk[:, sl]) * (XA_DH ** -0.5) for sl in sls]
    es = [jnp.exp(s - jnp.max(s, axis=-1, keepdims=True)) for s in ss]
    prs = [(e / jnp.sum(e, axis=-1, keepdims=True)).astype(BF16) for e in es]
    o = jnp.concatenate([_dot(pr, mv[:, sl]) for pr, sl in zip(prs, sls)], axis=1).astype(BF16)
    o_ref[...] = x + _dot(o, wo_ref[...])


def _xattn(x, l, gain, wq, wo, mk, mv):
    b, t, _ = x.shape
    tt = min(t, ROW_TILE)
    wspec = lambda shp: pl.BlockSpec((None,) + shp, lambda i, j: (l, 0, 0))
    return pl.pallas_call(
        _xattn_body,
        grid=(b, t // tt),
        in_specs=[
            pl.BlockSpec((None, tt, D_MODEL), lambda i, j: (i, j, 0)),
            wspec((1, D_MODEL)),
            wspec((D_MODEL, D_MODEL)),
            wspec((D_MODEL, D_MODEL)),
            pl.BlockSpec((None, None, N_MEM, D_MODEL), lambda i, j: (l, i, 0, 0)),
            pl.BlockSpec((None, None, N_MEM, D_MODEL), lambda i, j: (l, i, 0, 0)),
        ],
        out_specs=pl.BlockSpec((None, tt, D_MODEL), lambda i, j: (i, j, 0)),
        out_shape=jax.ShapeDtypeStruct(x.shape, F32),
        compiler_params=pltpu.CompilerParams(
            dimension_semantics=("arbitrary", "arbitrary"), vmem_limit_bytes=VMEM_LIMIT),
        name="xattn",
    )(x, gain, wq, wo, mk, mv)


def _prefix_matrix(c):
    nlev = int(math.log2(c))
    assert 1 << nlev == c
    tri = np.tril(np.ones((c, c), np.float32))
    return jnp.asarray(tri, dtype=BF16), nlev


def _upper_rows(x, b):
    return jnp.concatenate([x[s + b:s + 2 * b] for s in range(0, x.shape[0], 2 * b)], axis=0)


def _merge_rows(x, up, b):
    pieces = []
    for k, s in enumerate(range(0, x.shape[0], 2 * b)):
        pieces += [x[s:s + b], up[k * b:(k + 1) * b]]
    return jnp.concatenate(pieces, axis=0)


def _mix_body(x_ref, g_ref, win_ref, wst_ref, cw_ref, vec_ref, rowp_ref, gup_ref, gb_ref, gng_ref,
              gnl_ref, wout_ref, cs_ref, conv0_ref, sg0_ref, sl0_ref,
              y_ref, convo_ref, sgo_ref, slo_ref,
              cbuf, sgdn, sgla, *, tt, nlev):
    t = pl.program_id(1)
    nt = pl.num_programs(1)
    tail = SUBLANES - (CONV_W - 1)

    @pl.when(t == 0)
    def _():
        cbuf[tail:SUBLANES, :] = conv0_ref[...]
        sgdn[...] = sg0_ref[...]
        sgla[...] = sl0_ref[...]

    x = x_ref[...]
    h = _rms(x, g_ref[...]).astype(BF16)
    proj = lambda c0, c1: _dot(h, win_ref[:, c0:c1])
    sm = proj(C_SM, IN_COLS_PAD)
    st = _dot_nt(wst_ref[...], h)
    p_qk2 = proj(C_GQ, C_GV)
    cbuf[SUBLANES:SUBLANES + tt, :] = proj(C_QKV, C_Z)
    v2 = proj(C_GV, C_GR).astype(BF16)
    gate_z = _silu(proj(C_Z, C_GQ))
    gate_r = _silu(proj(C_GR, C_SM))

    cw = cw_ref[...]
    blocks = []
    for c in range(0, GDN_CONV_CH, LANES):
        cs_ = slice(c, c + LANES)
        conv = cbuf[tail:tail + tt, cs_] * cw[0:1, cs_]
        for i in range(1, CONV_W):
            conv = conv + cbuf[tail + i:tail + i + tt, cs_] * cw[i:i + 1, cs_]
        blocks.append(_silu(conv))
    new_tail = cbuf[tail + tt:SUBLANES + tt, :]
    cbuf[tail:SUBLANES, :] = new_tail
    nq = GDN_QK // LANES


    ii = lax.broadcasted_iota(jnp.int32, (tt, tt), 0)
    jj = lax.broadcasted_iota(jnp.int32, (tt, tt), 1)
    lower = ii > jj
    lv = jnp.where(lower, 31 - lax.clz(ii ^ jj), -1)
    eye = jnp.where(ii == jj, 1.0, 0.0).astype(F32)
    lfull = cs_ref[...]

    vec = vec_ref[...]
    beta_c = jax.nn.sigmoid(sm)
    g_c = vec[0:1, :] * _softplus(sm + vec[1:2, :])
    g_hi, g_lo = _split2(g_c)
    gcum_c = _dot(lfull, g_hi) + _dot(lfull, g_lo)
    rowp = rowp_ref[...]
    g_r = rowp[0] * _softplus(st + rowp[1])
    r_hi, r_lo = _split2(g_r)
    gcum_r = _dot_nt(r_hi, lfull) + _dot_nt(r_lo, lfull)

    xg = _dot(sm.astype(BF16), gup_ref[...]) + gb_ref[...]
    la = -_softplus(-xg) * (1.0 / GLA_TAU)
    la_hi, la_lo = _split2(la)
    row_id = lax.broadcasted_iota(jnp.int32, (tt, GLA_QK), 0)

    lane_head = lax.broadcasted_iota(jnp.int32, (1, GLA_QK), 1) // GLA_DK

    head_mask = [jnp.where(lane_head == hd, 1.0, 0.0).astype(BF16) for hd in range(GLA_HEADS)]

    def stack_heads(v):
        vb = v.astype(BF16)
        return jnp.concatenate([vb * m for m in head_mask], axis=0)

    q2 = p_qk2[:, :GLA_QK] * (GLA_DK ** -0.5)
    k2 = p_qk2[:, GLA_QK:]
    ii4 = jnp.concatenate([ii] * GLA_HEADS, axis=0)
    jj4 = jnp.concatenate([jj] * GLA_HEADS, axis=0)
    lv4 = jnp.concatenate([lv] * GLA_HEADS, axis=0)

    hs = []
    for hd in range(GDN_HEADS):
        qh, kh, vh = blocks[hd], blocks[nq + hd], blocks[2 * nq + hd]
        qh = qh * lax.rsqrt(jnp.sum(qh * qh, axis=-1, keepdims=True) + EPS) * (GDN_DK ** -0.5)
        kh = kh * lax.rsqrt(jnp.sum(kh * kh, axis=-1, keepdims=True) + EPS)
        bcol = beta_c[:, SM_B + hd:SM_B + hd + 1]
        gc = gcum_c[:, SM_A + hd:SM_A + hd + 1]
        gr = gcum_r[SM_A + hd:SM_A + hd + 1, :]
        dec = jnp.exp(jnp.minimum(gc - gr, 0.0))
        kb = kh * bcol
        kk = _dot_nt(jnp.concatenate([kb, qh], axis=0).astype(BF16), kh.astype(BF16))
        a = jnp.where(lower, kk[:tt] * dec, 0.0)
        qk = jnp.where(ii >= jj, kk[tt:] * dec, 0.0).astype(BF16)
        eg = jnp.exp(gc)
        glast = gr[:, tt - 1:tt]
        hs.append(dict(
            a=a, qk=qk, tinv=eye - jnp.where(lv == 0, a, 0.0),
            rhs=jnp.concatenate([vh * bcol, kb * eg], axis=1).astype(BF16),
            qg=(qh * eg).astype(BF16), kg=(kh * jnp.exp(glast - gc)).astype(BF16), eglast=jnp.exp(glast)))

    pm = jnp.where(ii4 == jj4, _dot_nt(stack_heads(q2), k2.astype(BF16)), 0.0)
    rq, tot = la, la
    for l in range(nlev):
        b = 1 << l
        sliced = b % SUBLANES == 0
        if l >= 1:
            ys = []
            for d in hs:
                d['tb'] = d['tinv'].astype(BF16)
                am = jnp.where(lv == l, d['a'], 0.0)
                if sliced:
                    y = _dot(_upper_rows(am, b).astype(BF16), d['tb'])
                    ys.append(_merge_rows(jnp.zeros_like(am), y, b).astype(BF16))
                else:
                    ys.append(_dot(am.astype(BF16), d['tb']).astype(BF16))
            for d, y in zip(hs, ys):
                if sliced:
                    tu = _upper_rows(d['tinv'], b)
                    d['tinv'] = _merge_rows(d['tinv'], tu - _dot(tu.astype(BF16), y), b)
                else:
                    d['tinv'] = d['tinv'] - _dot(d['tb'], y)
        qt = q2 * jnp.exp(rq)
        kt = (k2 * jnp.exp(tot - rq)).astype(BF16)
        if sliced:
            pu = _upper_rows(pm, b)
            r = _dot_nt(stack_heads(_upper_rows(qt, b)), kt)
            pm = _merge_rows(pm, jnp.where(_upper_rows(lv4, b) == l, r, pu), b)
        else:
            pm = jnp.where(lv4 == l, _dot_nt(stack_heads(qt), kt), pm)
        prev = pltpu.roll(tot, b, axis=0)
        nxt = pltpu.roll(tot, tt - b, axis=0)
        hi_half = (row_id & b) != 0
        rq = rq + jnp.where(hi_half, prev, 0.0)
        tot = tot + jnp.where(hi_half, prev, nxt)
    bcum, brev = rq, tot - rq

    outs = []
    uws = [_dot(d['tinv'].astype(BF16), d['rhs']) for d in hs]
    s_olds = [sgdn[hd] for hd in range(GDN_HEADS)]
    wqs = [_dot(jnp.concatenate([uw[:, GDN_DV:].astype(BF16), d['qg']], axis=0), s.astype(BF16))
           for d, uw, s in zip(hs, uws, s_olds)]
    unews = [(uw[:, :GDN_DV] - wq[:tt]).astype(BF16) for uw, wq in zip(uws, wqs)]
    for hd in range(GDN_HEADS):
        d = hs[hd]
        o = wqs[hd][tt:] + _dot(d['qk'], unews[hd])
        sgdn[hd] = s_olds[hd] * d['eglast'] + _dot_tn(d['kg'], unews[hd])
        on = _rms(o, gng_ref[...])
        outs.append(on * gate_z[:, hd * GDN_DV:(hd + 1) * GDN_DV])

    s2 = sgla[...]
    inter = _dot(stack_heads(q2 * jnp.exp(bcum)), s2.astype(BF16))
    ds = _dot_tn((k2 * jnp.exp(brev)).astype(BF16), v2)
    ones = jnp.ones((tt, GLA_DV), BF16)
    blast = _dot_tn(la_hi, ones) + _dot_tn(la_lo, ones)
    for hd in range(GLA_HEADS):
        vs = slice(hd * GLA_DV, (hd + 1) * GLA_DV)
        rs = slice(hd * GLA_DK, (hd + 1) * GLA_DK)
        o = _dot(pm[hd * tt:(hd + 1) * tt].astype(BF16), v2[:, vs]) + inter[hd * tt:(hd + 1) * tt]
        sgla[rs, :] = s2[rs, :] * jnp.exp(blast[rs, :]) + ds[rs, vs]
        on = _rms(o, gnl_ref[...])
        outs.append(on * gate_r[:, hd * GLA_DV:(hd + 1) * GLA_DV])

    o_all = jnp.concatenate(outs, axis=1).astype(BF16)
    y_ref[...] = x + _dot(o_all, wout_ref[...])

    @pl.when(t == nt - 1)
    def _():
        convo_ref[...] = new_tail
        sgo_ref[...] = sgdn[...]
        slo_ref[...] = sgla[...]


def _mix(x, l, W, conv0, sg0, sl0):
    b, t, _ = x.shape
    tt = min(t, MIX_TILE)
    nt = t // tt
    cs, nlev = _prefix_matrix(tt)
    rowp = jnp.broadcast_to(W['rowp'][l][:, :, None], (2, 2 * SUBLANES, tt))
    wspec = lambda shp: pl.BlockSpec((None,) + shp, lambda i, j: (l,) + (0,) * len(shp))
    full = lambda shp: pl.BlockSpec(shp, lambda i, j: (0,) * len(shp))
    sspec = lambda shp: pl.BlockSpec((None,) + shp, lambda i, j: (i,) + (0,) * len(shp))
    return pl.pallas_call(
        functools.partial(_mix_body, tt=tt, nlev=nlev),
        grid=(b, nt),
        in_specs=[
            pl.BlockSpec((None, tt, D_MODEL), lambda i, j: (i, j, 0)),
            wspec((1, D_MODEL)),
            wspec((D_MODEL, IN_COLS_PAD)),
            wspec((2 * SUBLANES, D_MODEL)),
            wspec((CONV_W, GDN_CONV_CH)),
            wspec((SUBLANES, LANES)),
            full((2, 2 * SUBLANES, tt)),
            wspec((LANES, GLA_QK)),
            wspec((1, GLA_QK)),
            wspec((1, GDN_DV)),
            wspec((1, GLA_DV)),
            wspec((MIX_W, D_MODEL)),
            full((tt, tt)),
            sspec((CONV_W - 1, GDN_CONV_CH)),
            sspec((GDN_HEADS, GDN_DK, GDN_DV)),
            sspec((GLA_QK, GLA_DV)),
        ],
        out_specs=[
            pl.BlockSpec((None, tt, D_MODEL), lambda i, j: (i, j, 0)),
            sspec((CONV_W - 1, GDN_CONV_CH)),
            sspec((GDN_HEADS, GDN_DK, GDN_DV)),
            sspec((GLA_QK, GLA_DV)),
        ],
        out_shape=[
            jax.ShapeDtypeStruct(x.shape, F32),
            jax.ShapeDtypeStruct((b, CONV_W - 1, GDN_CONV_CH), F32),
            jax.ShapeDtypeStruct((b, GDN_HEADS, GDN_DK, GDN_DV), F32),
            jax.ShapeDtypeStruct((b, GLA_QK, GLA_DV), F32),
        ],
        scratch_shapes=[
            pltpu.VMEM((SUBLANES + tt, GDN_CONV_CH), F32),
            pltpu.VMEM((GDN_HEADS, GDN_DK, GDN_DV), F32),
            pltpu.VMEM((GLA_QK, GLA_DV), F32),
        ],
        compiler_params=pltpu.CompilerParams(
            dimension_semantics=("arbitrary", "arbitrary"), vmem_limit_bytes=VMEM_LIMIT),
        name="mix",
    )(x, W['mix_norm'], W['w_in'], W['w_st'], W['gdn_conv_w'], W['vec'], rowp, W['gup'], W['gla_gate_bias'],
      W['gdn_out_norm'], W['gla_out_norm'], W['w_out'], cs, conv0, sg0, sl0)


def _prep_weights(ffn1_norm, ffn1_w_gate, ffn1_w_up, ffn1_w_down, mix_norm, w_in, gdn_conv_w, gdn_a_log,
                  gdn_dt_bias, gdn_out_norm, gla_gate_up, gla_gate_bias, gla_out_norm, w_out, xattn_norm,
                  xattn_w_q, xattn_w_o, ffn2_norm, ffn2_w_gate, ffn2_w_up, ffn2_w_down, final_norm):
    o = IN_OFFSETS
    seg = lambda k: w_in[:, :, o[k]:o[k] + IN_SIZES[k]]
    small = jnp.concatenate([seg(1), seg(2), seg(8)], axis=-1)
    small = jnp.pad(small, ((0, 0), (0, 0), (0, LANES - small.shape[-1])))
    w_in_r = jnp.concatenate([seg(0), seg(3), seg(4), seg(5), seg(6), seg(7), small], axis=-1).astype(BF16)
    w_st = jnp.concatenate([seg(1), seg(2)], axis=-1)
    w_st = jnp.pad(jnp.swapaxes(w_st, 1, 2), ((0, 0), (0, SUBLANES), (0, 0))).astype(BF16)
    nega = -jnp.exp(gdn_a_log.astype(F32))
    dtb = gdn_dt_bias.astype(F32)
    lane_pad = lambda v: jnp.pad(v, ((0, 0), (SM_A, LANES - SM_A - GDN_HEADS)))
    vec = jnp.stack([lane_pad(nega), lane_pad(dtb)], axis=1)
    vec = jnp.pad(vec, ((0, 0), (0, SUBLANES - 2), (0, 0)))
    row_pad = lambda v: jnp.pad(v, ((0, 0), (SM_A, 2 * SUBLANES - SM_A - GDN_HEADS)))
    rowp = jnp.stack([row_pad(nega), row_pad(dtb)], axis=1)
    gup = jnp.pad(gla_gate_up, ((0, 0), (SM_R, LANES - SM_R - GLA_RANK), (0, 0))).astype(BF16)
    r3 = lambda v: v[:, None, :].astype(F32)
    return dict(
        ffn1_norm=r3(ffn1_norm), ffn1_w_gate=ffn1_w_gate.astype(BF16), ffn1_w_up=ffn1_w_up.astype(BF16),
        ffn1_w_down=ffn1_w_down.astype(BF16), mix_norm=r3(mix_norm), w_in=w_in_r, w_st=w_st,
        gdn_conv_w=gdn_conv_w.astype(F32), vec=vec, rowp=rowp, gup=gup, gla_gate_bias=r3(gla_gate_bias),
        gdn_out_norm=r3(gdn_out_norm), gla_out_norm=r3(gla_out_norm), w_out=w_out.astype(BF16),
        xattn_norm=r3(xattn_norm), xattn_w_q=xattn_w_q.astype(BF16), xattn_w_o=xattn_w_o.astype(BF16),
        ffn2_norm=r3(ffn2_norm), ffn2_w_gate=ffn2_w_gate.astype(BF16), ffn2_w_up=ffn2_w_up.astype(BF16),
        ffn2_w_down=ffn2_w_down.astype(BF16), final_norm=final_norm[None, :].astype(F32))


def _trunk(x, mem_k, mem_v, conv_state, gdn_state, gla_state, W):
    b, t, _ = x.shape
    mk = mem_k.reshape(DEPTH, b, N_MEM, D_MODEL)
    mv = mem_v.reshape(DEPTH, b, N_MEM, D_MODEL)
    gla_state = gla_state.reshape(DEPTH, b, GLA_QK, GLA_DV)
    convs, gdns, glas = [], [], []
    for l in range(DEPTH):
        x2 = _ffn(x.reshape(b * t, D_MODEL), l, W['ffn1_norm'], W['ffn1_w_gate'], W['ffn1_w_up'],
                  W['ffn1_w_down'], W['final_norm'], False)
        x, cb, sa, sb = _mix(x2.reshape(b, t, D_MODEL), l, W, conv_state[l], gdn_state[l], gla_state[l])
        x = _xattn(x, l, W['xattn_norm'], W['xattn_w_q'], W['xattn_w_o'], mk, mv)
        x2 = _ffn(x.reshape(b * t, D_MODEL), l, W['ffn2_norm'], W['ffn2_w_gate'], W['ffn2_w_up'],
                  W['ffn2_w_down'], W['final_norm'], l == DEPTH - 1)
        x = x2.reshape(b, t, D_MODEL)
        convs.append(cb)
        gdns.append(sa)
        glas.append(sb.reshape(b, GLA_HEADS, GLA_DK, GLA_DV))
    return x, jnp.stack(convs), jnp.stack(gdns), jnp.stack(glas)


def kernel(x_prompt, x_sample, mem_prompt, cache_mem_k, cache_mem_v, state_gdn_conv, state_gdn, state_gla, ffn1_norm, ffn1_w_gate, ffn1_w_up, ffn1_w_down, mix_norm, w_in, gdn_conv_w, gdn_a_log, gdn_dt_bias, gdn_out_norm, gla_gate_up, gla_gate_bias, gla_out_norm, w_out, xattn_norm, xattn_w_q, xattn_w_k, xattn_w_v, xattn_w_o, ffn2_norm, ffn2_w_gate, ffn2_w_up, ffn2_w_down, final_norm):
    W = _prep_weights(ffn1_norm, ffn1_w_gate, ffn1_w_up, ffn1_w_down, mix_norm, w_in, gdn_conv_w, gdn_a_log,
                      gdn_dt_bias, gdn_out_norm, gla_gate_up, gla_gate_bias, gla_out_norm, w_out, xattn_norm,
                      xattn_w_q, xattn_w_o, ffn2_norm, ffn2_w_gate, ffn2_w_up, ffn2_w_down, final_norm)
    bp = x_prompt.shape[0]
    mem2 = mem_prompt.reshape(bp * N_MEM, D_MODEL).astype(BF16)
    mk_p, mv_p = _memkv(mem2, xattn_w_k.astype(BF16), xattn_w_v.astype(BF16))
    mem_k_p = mk_p.reshape(DEPTH, bp, N_MEM, XA_HEADS, XA_DH)
    mem_v_p = mv_p.reshape(DEPTH, bp, N_MEM, XA_HEADS, XA_DH)
    conv0 = jnp.zeros((DEPTH, bp, CONV_W - 1, GDN_CONV_CH), F32)
    gdn0 = jnp.zeros((DEPTH, bp, GDN_HEADS, GDN_DK, GDN_DV), F32)
    gla0 = jnp.zeros((DEPTH, bp, GLA_HEADS, GLA_DK, GLA_DV), F32)
    y_p, conv_p, gdn_p, gla_p = _trunk(x_prompt, mem_k_p, mem_v_p, conv0, gdn0, gla0, W)
    y_s, conv_s, gdn_s, gla_s = _trunk(x_sample, cache_mem_k, cache_mem_v, state_gdn_conv, state_gdn,
                                       state_gla, W)
    return (y_p, y_s, conv_p, gdn_p, gla_p, mem_k_p, mem_v_p, conv_s, gdn_s, gla_s)
```

```python
import functools
import math

import numpy as np
import jax
import jax.numpy as jnp
from jax import lax
from jax.experimental import pallas as pl
from jax.experimental.pallas import tpu as pltpu

F32 = jnp.float32
BF16 = jnp.bfloat16

D_MODEL = 1024
DEPTH = 4
EPS = 1e-6
N_MEM = 256
CONV_W = 4
GDN_HEADS, GDN_DK, GDN_DV = 4, 128, 128
GDN_QK = GDN_HEADS * GDN_DK
GDN_V = GDN_HEADS * GDN_DV
GDN_CONV_CH = 2 * GDN_QK + GDN_V
GLA_HEADS, GLA_DK, GLA_DV = 4, 64, 128
GLA_RANK = 16
GLA_TAU = 16.0
GLA_QK = GLA_HEADS * GLA_DK
GLA_V = GLA_HEADS * GLA_DV
MIX_W = GDN_V + GLA_V
XA_HEADS = 4
XA_DH = D_MODEL // XA_HEADS
D_FF = 2816
IN_SIZES = (GDN_CONV_CH, GDN_HEADS, GDN_HEADS, GDN_V, GLA_QK, GLA_QK, GLA_V, GLA_V, GLA_RANK)
IN_OFFSETS = tuple(int(o) for o in np.cumsum((0,) + IN_SIZES)[:-1])

LANES = 128
SUBLANES = 8
C_QKV = 0
C_Z = C_QKV + GDN_CONV_CH
C_GQ = C_Z + GDN_V
C_GK = C_GQ + GLA_QK
C_GV = C_GK + GLA_QK
C_GR = C_GV + GLA_V
C_SM = C_GR + GLA_V
IN_COLS_PAD = C_SM + LANES
SM_B, SM_A, SM_R = 0, GDN_HEADS, 2 * GDN_HEADS

VMEM_LIMIT = 56 * 1024 * 1024
MIX_TILE = 256
MIX_ROWS = 2
ROW_TILE = 512
FFN_HALVES = 2
FF_CHUNK = 256


def _dot(a, b):
    return jnp.dot(a, b, preferred_element_type=F32)


def _dot_nt(a, b):
    return lax.dot_general(a, b, (((1,), (1,)), ((), ())), preferred_element_type=F32)


def _dot_tn(a, b):
    return lax.dot_general(a, b, (((0,), (0,)), ((), ())), preferred_element_type=F32)


def _rms(x, g):
    return x * lax.rsqrt(jnp.mean(x * x, axis=-1, keepdims=True) + EPS) * g


def _silu(x):
    return x * jax.nn.sigmoid(x)


def _softplus(x):
    return jnp.maximum(x, 0.0) + jnp.log(1.0 + jnp.exp(-jnp.abs(x)))


def _split2(x):
    hi = x.astype(BF16)
    lo = (x - hi.astype(F32)).astype(BF16)
    return hi, lo


def _ffn_body(x_ref, g_ref, wg_ref, wu_ref, wd_ref, fg_ref, o_ref, a_ref, *, final, halves):
    tm = x_ref.shape[0] // halves
    hs = {}

    def norm(i):
        hs[i] = _rms(x_ref[i * tm:(i + 1) * tm, :], g_ref[...]).astype(BF16)

    norm(0)
    for i in range(halves):
        rows = slice(i * tm, (i + 1) * tm)
        for k, f in enumerate(range(0, D_FF, FF_CHUNK)):
            gt = _dot(hs[i], wg_ref[:, f:f + FF_CHUNK])
            up = _dot(hs[i], wu_ref[:, f:f + FF_CHUNK])
            a_ref[rows, f:f + FF_CHUNK] = (_silu(gt) * up).astype(BF16)
            if k == 0 and i + 1 < halves:
                norm(i + 1)
        y = x_ref[rows, :] + 0.5 * _dot(a_ref[rows, :], wd_ref[...])
        if final:
            y = _rms(y, fg_ref[...])
        o_ref[rows, :] = y


def _ffn(x2, l, gain, wg, wu, wd, fgain, final):
    m = x2.shape[0]
    tm = min(m, FFN_HALVES * ROW_TILE)
    halves = FFN_HALVES if tm == FFN_HALVES * ROW_TILE else 1
    wspec = lambda shp: pl.BlockSpec((None,) + shp, lambda i: (l, 0, 0), pipeline_mode=pl.Buffered(1))
    return pl.pallas_call(
        functools.partial(_ffn_body, final=final, halves=halves),
        grid=(m // tm,),
        in_specs=[
            pl.BlockSpec((tm, D_MODEL), lambda i: (i, 0)),
            wspec((1, D_MODEL)),
            wspec((D_MODEL, D_FF)),
            wspec((D_MODEL, D_FF)),
            wspec((D_FF, D_MODEL)),
            pl.BlockSpec((1, D_MODEL), lambda i: (0, 0)),
        ],
        out_specs=pl.BlockSpec((tm, D_MODEL), lambda i: (i, 0)),
        out_shape=jax.ShapeDtypeStruct((m, D_MODEL), F32),
        scratch_shapes=[pltpu.VMEM((tm, D_FF), BF16)],
        compiler_params=pltpu.CompilerParams(
            dimension_semantics=("arbitrary",), vmem_limit_bytes=VMEM_LIMIT),
        name="ffn",
    )(x2, gain, wg, wu, wd, fgain)


def _memkv_body(m_ref, wk_ref, wv_ref, k_ref, v_ref):
    m = m_ref[...]
    k_ref[...] = _dot(m, wk_ref[...])
    v_ref[...] = _dot(m, wv_ref[...])


def _memkv(mem2, wk, wv):
    m = mem2.shape[0]
    tm = min(m, 1024)
    return pl.pallas_call(
        _memkv_body,
        grid=(DEPTH, m // tm),
        in_specs=[
            pl.BlockSpec((tm, D_MODEL), lambda l, i: (i, 0)),
            pl.BlockSpec((None, D_MODEL, D_MODEL), lambda l, i: (l, 0, 0)),
            pl.BlockSpec((None, D_MODEL, D_MODEL), lambda l, i: (l, 0, 0)),
        ],
        out_specs=[
            pl.BlockSpec((None, tm, D_MODEL), lambda l, i: (l, i, 0)),
            pl.BlockSpec((None, tm, D_MODEL), lambda l, i: (l, i, 0)),
        ],
        out_shape=[jax.ShapeDtypeStruct((DEPTH, m, D_MODEL), F32)] * 2,
        compiler_params=pltpu.CompilerParams(
            dimension_semantics=("arbitrary", "arbitrary"), vmem_limit_bytes=VMEM_LIMIT),
        name="memkv",
    )(mem2, wk, wv)


def _xattn_body(x_ref, g_ref, wq_ref, wo_ref, mk_ref, mv_ref, o_ref):
    x = x_ref[...]
    h = _rms(x, g_ref[...]).astype(BF16)
    q = _dot(h, wq_ref[...])
    mk = mk_ref[...].astype(BF16)
    mv = mv_ref[...].astype(BF16)
    sls = [slice(hh * XA_DH, (hh + 1) * XA_DH) for hh in range(XA_HEADS)]
    ss = [_dot_nt(q[:, sl].astype(BF16), mk[:, sl]) * (XA_DH ** -0.5) for sl in sls]
    es = [jnp.exp(s - jnp.max(s, axis=-1, keepdims=True)) for s in ss]
    prs = [(e / jnp.sum(e, axis=-1, keepdims=True)).astype(BF16) for e in es]
    o = jnp.concatenate([_dot(pr, mv[:, sl]) for pr, sl in zip(prs, sls)], axis=1).astype(BF16)
    o_ref[...] = x + _dot(o, wo_ref[...])


def _xattn(x, l, gain, wq, wo, mk, mv):
    b, t, _ = x.shape
    tt = min(t, ROW_TILE)
    wspec = lambda shp: pl.BlockSpec((None,) + shp, lambda i, j: (l, 0, 0))
    return pl.pallas_call(
        _xattn_body,
        grid=(b, t // tt),
        in_specs=[
            pl.BlockSpec((None, tt, D_MODEL), lambda i, j: (i, j, 0)),
            wspec((1, D_MODEL)),
            wspec((D_MODEL, D_MODEL)),
            wspec((D_MODEL, D_MODEL)),
            pl.BlockSpec((None, None, N_MEM, D_MODEL), lambda i, j: (l, i, 0, 0)),
            pl.BlockSpec((None, None, N_MEM, D_MODEL), lambda i, j: (l, i, 0, 0)),
        ],
        out_specs=pl.BlockSpec((None, tt, D_MODEL), lambda i, j: (i, j, 0)),
        out_shape=jax.ShapeDtypeStruct(x.shape, F32),
        compiler_params=pltpu.CompilerParams(
            dimension_semantics=("arbitrary", "arbitrary"), vmem_limit_bytes=VMEM_LIMIT),
        name="xattn",
    )(x, gain, wq, wo, mk, mv)


def _prefix_matrix(c):
    nlev = int(math.log2(c))
    assert 1 << nlev == c
    tri = np.tril(np.ones((c, c), np.float32))
    return jnp.asarray(tri, dtype=BF16), nlev


def _upper_rows(x, b):
    return jnp.concatenate([x[s + b:s + 2 * b] for s in range(0, x.shape[0], 2 * b)], axis=0)


def _merge_rows(x, up, b):
    pieces = []
    for k, s in enumerate(range(0, x.shape[0], 2 * b)):
        pieces += [x[s:s + b], up[k * b:(k + 1) * b]]
    return jnp.concatenate(pieces, axis=0)


def _load_upper(ref, b):
    return jnp.concatenate([ref[s + b:s + 2 * b, :] for s in range(0, ref.shape[0], 2 * b)], axis=0)


def _store_upper(ref, up, b):
    for k, s in enumerate(range(0, ref.shape[0], 2 * b)):
        ref[s + b:s + 2 * b, :] = up[k * b:(k + 1) * b]


def _alternate(*gens):
    live = list(gens)
    while live:
        for g in list(live):
            if next(g, StopIteration) is StopIteration:
                live.remove(g)


def _mix_body(x_ref, g_ref, win_ref, wst_ref, cw_ref, vec_ref, rowp_ref, gup_ref, gb_ref, gng_ref,
              gnl_ref, wout_ref, cs_ref, conv0_ref, sg0_ref, sl0_ref,
              y_ref, convo_ref, sgo_ref, slo_ref,
              cbuf, sgdn, sgla, a_s, tinv_s, pm_s, *, tt, nlev, rows):
    t = pl.program_id(1)
    nt = pl.num_programs(1)
    tail = SUBLANES - (CONV_W - 1)

    @pl.when(t == 0)
    def _():
        cbuf[...] = jnp.zeros(cbuf.shape, F32)
        cbuf[:, tail:SUBLANES, :] = conv0_ref[...]
        sgdn[...] = sg0_ref[...]
        sgla[...] = sl0_ref[...]

    ii = lax.broadcasted_iota(jnp.int32, (tt, tt), 0)
    jj = lax.broadcasted_iota(jnp.int32, (tt, tt), 1)
    lower = ii > jj
    lv = jnp.where(lower, 31 - lax.clz(ii ^ jj), -1)
    eye = jnp.where(ii == jj, 1.0, 0.0).astype(F32)
    ii4 = jnp.concatenate([ii] * GLA_HEADS, axis=0)
    jj4 = jnp.concatenate([jj] * GLA_HEADS, axis=0)
    lv4 = jnp.concatenate([lv] * GLA_HEADS, axis=0)
    row_id = lax.broadcasted_iota(jnp.int32, (tt, GLA_QK), 0)
    lane_head = lax.broadcasted_iota(jnp.int32, (1, GLA_QK), 1) // GLA_DK
    head_mask = [jnp.where(lane_head == hd, 1.0, 0.0).astype(BF16) for hd in range(GLA_HEADS)]
    nq = GDN_QK // LANES

    def stack_heads(v):
        vb = v.astype(BF16)
        return jnp.concatenate([vb * m for m in head_mask], axis=0)

    def project(r, c):
        x = x_ref[r]
        h = _rms(x, g_ref[...]).astype(BF16)
        proj = lambda c0, c1: _dot(h, win_ref[:, c0:c1])
        c['x'] = x
        c['sm'] = proj(C_SM, IN_COLS_PAD)
        c['st'] = _dot_nt(wst_ref[...], h)
        yield
        c['p_qk2'] = proj(C_GQ, C_GV)
        yield
        c['p_qkv'] = []
        for c0 in range(C_QKV, C_Z, GDN_QK):
            c['p_qkv'].append(proj(c0, c0 + GDN_QK))
            yield
        c['v2'] = proj(C_GV, C_GR).astype(BF16)
        yield
        c['gate_z'] = _silu(proj(C_Z, C_GQ))
        yield
        c['gate_r'] = _silu(proj(C_GR, C_SM))
        yield

    def front(r, c):
        sm, st, p_qk2 = c['sm'], c['st'], c['p_qk2']

        cw = cw_ref[...]
        blocks, tails = [], []
        for cc in range(0, GDN_CONV_CH, LANES):
            cs_ = slice(cc, cc + LANES)
            seg = c['p_qkv'][cc // GDN_QK][:, cc % GDN_QK:cc % GDN_QK + LANES]
            ext = jnp.concatenate([cbuf[r, :, cs_], seg], axis=0)
            conv = ext[tail:tail + tt] * cw[0:1, cs_]
            for i in range(1, CONV_W):
                conv = conv + ext[tail + i:tail + i + tt] * cw[i:i + 1, cs_]
            blocks.append(_silu(conv))
            tails.append(seg[tt - SUBLANES:tt])
            if len(blocks) % 3 == 0:
                yield
        last = jnp.concatenate(tails, axis=1)
        cbuf[r] = last
        c['new_tail'] = last[tail:SUBLANES]

        lfull = cs_ref[...]
        vec = vec_ref[...]
        beta_c = jax.nn.sigmoid(sm)
        g_c = vec[0:1, :] * _softplus(sm + vec[1:2, :])
        g_hi, g_lo = _split2(g_c)
        gcum_c = _dot(lfull, g_hi) + _dot(lfull, g_lo)
        rowp = rowp_ref[...]
        g_r = rowp[0] * _softplus(st + rowp[1])
        r_hi, r_lo = _split2(g_r)
        gcum_r = _dot_nt(r_hi, lfull) + _dot_nt(r_lo, lfull)

        xg = _dot(sm.astype(BF16), gup_ref[...]) + gb_ref[...]
        c['la'] = -_softplus(-xg) * (1.0 / GLA_TAU)
        c['q2'] = p_qk2[:, :GLA_QK] * (GLA_DK ** -0.5)
        c['k2'] = p_qk2[:, GLA_QK:]
        yield

        hs = []
        for hd in range(GDN_HEADS):
            qh, kh, vh = blocks[hd], blocks[nq + hd], blocks[2 * nq + hd]
            qh = qh * lax.rsqrt(jnp.sum(qh * qh, axis=-1, keepdims=True) + EPS) * (GDN_DK ** -0.5)
            kh = kh * lax.rsqrt(jnp.sum(kh * kh, axis=-1, keepdims=True) + EPS)
            bcol = beta_c[:, SM_B + hd:SM_B + hd + 1]
            gc = gcum_c[:, SM_A + hd:SM_A + hd + 1]
            gr = gcum_r[SM_A + hd:SM_A + hd + 1, :]
            dec = jnp.exp(jnp.minimum(gc - gr, 0.0))
            kb = kh * bcol
            kk = _dot_nt(jnp.concatenate([kb, qh], axis=0).astype(BF16), kh.astype(BF16))
            a = jnp.where(lower, kk[:tt] * dec, 0.0)
            qk = jnp.where(ii >= jj, kk[tt:] * dec, 0.0).astype(BF16)
            eg = jnp.exp(gc)
            glast = gr[:, tt - 1:tt]
            a_s[r, hd] = a
            tinv_s[r, hd] = eye - jnp.where(lv == 0, a, 0.0)
            hs.append(dict(
                qk=qk,
                rhs=jnp.concatenate([vh * bcol, kb * eg], axis=1).astype(BF16),
                qg=(qh * eg).astype(BF16), kg=(kh * jnp.exp(glast - gc)).astype(BF16),
                eglast=jnp.exp(glast)))
            yield
        c['hs'] = hs

    def back(r, c):
        hs, q2, k2, v2, la = c['hs'], c['q2'], c['k2'], c['v2'], c['la']
        pm = pm_s.at[r]
        pm[...] = jnp.where(ii4 == jj4, _dot_nt(stack_heads(q2), k2.astype(BF16)), 0.0)
        rq, tot = la, la
        for l in range(nlev):
            b = 1 << l
            sliced = b % SUBLANES == 0
            if l >= 1:
                ys, tbs = [], []
                for hd in range(GDN_HEADS):
                    tb = tinv_s[r, hd].astype(BF16)
                    tbs.append(tb)
                    if sliced:
                        am = jnp.where(_upper_rows(lv, b) == l, _load_upper(a_s.at[r, hd], b), 0.0)
                        y = _dot(am.astype(BF16), tb).astype(BF16)
                        ys.append(_merge_rows(jnp.zeros((tt, tt), BF16), y, b))
                    else:
                        am = jnp.where(lv == l, a_s[r, hd], 0.0)
                        ys.append(_dot(am.astype(BF16), tb).astype(BF16))
                for hd in range(GDN_HEADS):
                    if sliced:
                        tu = _load_upper(tinv_s.at[r, hd], b)
                        _store_upper(tinv_s.at[r, hd], tu - _dot(tu.astype(BF16), ys[hd]), b)
                    else:
                        tinv_s[r, hd] = tinv_s[r, hd] - _dot(tbs[hd], ys[hd])
            qt = q2 * jnp.exp(rq)
            kt = (k2 * jnp.exp(tot - rq)).astype(BF16)
            if sliced:
                rr = _dot_nt(stack_heads(_upper_rows(qt, b)), kt)
                _store_upper(pm, jnp.where(_upper_rows(lv4, b) == l, rr, _load_upper(pm, b)), b)
            else:
                pm[...] = jnp.where(lv4 == l, _dot_nt(stack_heads(qt), kt), pm[...])
            prev = pltpu.roll(tot, b, axis=0)
            nxt = pltpu.roll(tot, tt - b, axis=0)
            hi_half = (row_id & b) != 0
            rq = rq + jnp.where(hi_half, prev, 0.0)
            tot = tot + jnp.where(hi_half, prev, nxt)
            yield
        bcum, brev = rq, tot - rq

        outs = []
        uws = [_dot(tinv_s[r, hd].astype(BF16), hs[hd]['rhs']) for hd in range(GDN_HEADS)]
        s_olds = [sgdn[r, hd] for hd in range(GDN_HEADS)]
        wqs = []
        zs = jnp.zeros((GDN_DK, GDN_DV), BF16)
        for ha in range(0, GDN_HEADS, 2):
            hb = ha + 1
            lhs = jnp.concatenate([
                jnp.concatenate([uws[ha][:, GDN_DV:], uws[hb][:, GDN_DV:]], axis=1).astype(BF16),
                jnp.concatenate([hs[ha]['qg'], hs[hb]['qg']], axis=1)], axis=0)
            sbd = jnp.concatenate([
                jnp.concatenate([s_olds[ha].astype(BF16), zs], axis=1),
                jnp.concatenate([zs, s_olds[hb].astype(BF16)], axis=1)], axis=0)
            res = _dot(lhs, sbd)
            wqs += [res[:, :GDN_DV], res[:, GDN_DV:]]
        unews = [(uw[:, :GDN_DV] - wq[:tt]).astype(BF16) for uw, wq in zip(uws, wqs)]
        yield
        for hd in range(GDN_HEADS):
            d = hs[hd]
            o = wqs[hd][tt:] + _dot(d['qk'], unews[hd])
            sgdn[r, hd] = s_olds[hd] * d['eglast'] + _dot_tn(d['kg'], unews[hd])
            on = _rms(o, gng_ref[...])
            outs.append(on * c['gate_z'][:, hd * GDN_DV:(hd + 1) * GDN_DV])
        yield

        s2 = sgla[r]
        row_head = lax.broadcasted_iota(jnp.int32, (GLA_QK, 1), 0) // GLA_DK
        s_bd = jnp.concatenate([jnp.where(row_head == hd, s2, 0.0) for hd in range(GLA_HEADS)],
                               axis=1).astype(BF16)
        inter = _dot((q2 * jnp.exp(bcum)).astype(BF16), s_bd)
        ds = _dot_tn((k2 * jnp.exp(brev)).astype(BF16), v2)
        la_hi, la_lo = _split2(la)
        ones = jnp.ones((tt, GLA_DV), BF16)
        blast = _dot_tn(la_hi, ones) + _dot_tn(la_lo, ones)
        yield
        for hd in range(GLA_HEADS):
            vs = slice(hd * GLA_DV, (hd + 1) * GLA_DV)
            rs = slice(hd * GLA_DK, (hd + 1) * GLA_DK)
            o = _dot(pm[hd * tt:(hd + 1) * tt, :].astype(BF16), v2[:, vs]) + inter[:, vs]
            sgla[r, rs, :] = s2[rs, :] * jnp.exp(blast[rs, :]) + ds[rs, vs]
            on = _rms(o, gnl_ref[...])
            outs.append(on * c['gate_r'][:, hd * GLA_DV:(hd + 1) * GLA_DV])
        yield

        o_all = jnp.concatenate(outs, axis=1).astype(BF16)
        y_ref[r] = c['x'] + _dot(o_all, wout_ref[...])
        yield

    ctx = [dict() for _ in range(rows)]
    stages = [g(r, ctx[r]) for r in range(rows) for g in (project, front, back)]
    order = [[stages[0]]]
    for r in range(rows):
        if r + 1 < rows:
            order.append([stages[3 * r + 1], stages[3 * r + 3]])
        else:
            order.append([stages[3 * r + 1]])
        if r >= 1:
            order[-1].insert(0, stages[3 * r - 1])
    order.append([stages[3 * rows - 1]])
    for group in order:
        if len(group) == 1:
            for _ in group[0]:
                pass
        else:
            _alternate(*group)

    @pl.when(t == nt - 1)
    def _():
        for r in range(rows):
            convo_ref[r] = ctx[r]['new_tail']
        sgo_ref[...] = sgdn[...]
        slo_ref[...] = sgla[...]


def _mix(x, l, W, conv0, sg0, sl0):
    b, t, _ = x.shape
    tt = min(t, MIX_TILE)
    nt = t // tt
    rows = MIX_ROWS if b % MIX_ROWS == 0 else 1
    cs, nlev = _prefix_matrix(tt)
    rowp = jnp.broadcast_to(W['rowp'][l][:, :, None], (2, 2 * SUBLANES, tt))
    wspec = lambda shp: pl.BlockSpec((None,) + shp, lambda i, j: (l,) + (0,) * len(shp))
    full = lambda shp: pl.BlockSpec(shp, lambda i, j: (0,) * len(shp))
    sspec = lambda shp: pl.BlockSpec((rows,) + shp, lambda i, j: (i,) + (0,) * len(shp))
    return pl.pallas_call(
        functools.partial(_mix_body, tt=tt, nlev=nlev, rows=rows),
        grid=(b // rows, nt),
        in_specs=[
            pl.BlockSpec((rows, tt, D_MODEL), lambda i, j: (i, j, 0)),
            wspec((1, D_MODEL)),
            wspec((D_MODEL, IN_COLS_PAD)),
            wspec((2 * SUBLANES, D_MODEL)),
            wspec((CONV_W, GDN_CONV_CH)),
            wspec((SUBLANES, LANES)),
            full((2, 2 * SUBLANES, tt)),
            wspec((LANES, GLA_QK)),
            wspec((1, GLA_QK)),
            wspec((1, GDN_DV)),
            wspec((1, GLA_DV)),
            wspec((MIX_W, D_MODEL)),
            full((tt, tt)),
            sspec((CONV_W - 1, GDN_CONV_CH)),
            sspec((GDN_HEADS, GDN_DK, GDN_DV)),
            sspec((GLA_QK, GLA_DV)),
        ],
        out_specs=[
            pl.BlockSpec((rows, tt, D_MODEL), lambda i, j: (i, j, 0)),
            sspec((CONV_W - 1, GDN_CONV_CH)),
            sspec((GDN_HEADS, GDN_DK, GDN_DV)),
            sspec((GLA_QK, GLA_DV)),
        ],
        out_shape=[
            jax.ShapeDtypeStruct(x.shape, F32),
            jax.ShapeDtypeStruct((b, CONV_W - 1, GDN_CONV_CH), F32),
            jax.ShapeDtypeStruct((b, GDN_HEADS, GDN_DK, GDN_DV), F32),
            jax.ShapeDtypeStruct((b, GLA_QK, GLA_DV), F32),
        ],
        scratch_shapes=[
            pltpu.VMEM((rows, SUBLANES, GDN_CONV_CH), F32),
            pltpu.VMEM((rows, GDN_HEADS, GDN_DK, GDN_DV), F32),
            pltpu.VMEM((rows, GLA_QK, GLA_DV), F32),
            pltpu.VMEM((rows, GDN_HEADS, tt, tt), F32),
            pltpu.VMEM((rows, GDN_HEADS, tt, tt), F32),
            pltpu.VMEM((rows, GLA_HEADS * tt, tt), F32),
        ],
        compiler_params=pltpu.CompilerParams(
            dimension_semantics=("arbitrary", "arbitrary"), vmem_limit_bytes=VMEM_LIMIT),
        name="mix",
    )(x, W['mix_norm'], W['w_in'], W['w_st'], W['gdn_conv_w'], W['vec'], rowp, W['gup'], W['gla_gate_bias'],
      W['gdn_out_norm'], W['gla_out_norm'], W['w_out'], cs, conv0, sg0, sl0)


def _prep_weights(ffn1_norm, ffn1_w_gate, ffn1_w_up, ffn1_w_down, mix_norm, w_in, gdn_conv_w, gdn_a_log,
                  gdn_dt_bias, gdn_out_norm, gla_gate_up, gla_gate_bias, gla_out_norm, w_out, xattn_norm,
                  xattn_w_q, xattn_w_o, ffn2_norm, ffn2_w_gate, ffn2_w_up, ffn2_w_down, final_norm):
    o = IN_OFFSETS
    seg = lambda k: w_in[:, :, o[k]:o[k] + IN_SIZES[k]]
    small = jnp.concatenate([seg(1), seg(2), seg(8)], axis=-1)
    small = jnp.pad(small, ((0, 0), (0, 0), (0, LANES - small.shape[-1])))
    w_in_r = jnp.concatenate([seg(0), seg(3), seg(4), seg(5), seg(6), seg(7), small], axis=-1).astype(BF16)
    w_st = jnp.concatenate([seg(1), seg(2)], axis=-1)
    w_st = jnp.pad(jnp.swapaxes(w_st, 1, 2), ((0, 0), (0, SUBLANES), (0, 0))).astype(BF16)
    nega = -jnp.exp(gdn_a_log.astype(F32))
    dtb = gdn_dt_bias.astype(F32)
    lane_pad = lambda v: jnp.pad(v, ((0, 0), (SM_A, LANES - SM_A - GDN_HEADS)))
    vec = jnp.stack([lane_pad(nega), lane_pad(dtb)], axis=1)
    vec = jnp.pad(vec, ((0, 0), (0, SUBLANES - 2), (0, 0)))
    row_pad = lambda v: jnp.pad(v, ((0, 0), (SM_A, 2 * SUBLANES - SM_A - GDN_HEADS)))
    rowp = jnp.stack([row_pad(nega), row_pad(dtb)], axis=1)
    gup = jnp.pad(gla_gate_up, ((0, 0), (SM_R, LANES - SM_R - GLA_RANK), (0, 0))).astype(BF16)
    r3 = lambda v: v[:, None, :].astype(F32)
    return dict(
        ffn1_norm=r3(ffn1_norm), ffn1_w_gate=ffn1_w_gate.astype(BF16), ffn1_w_up=ffn1_w_up.astype(BF16),
        ffn1_w_down=ffn1_w_down.astype(BF16), mix_norm=r3(mix_norm), w_in=w_in_r, w_st=w_st,
        gdn_conv_w=gdn_conv_w.astype(F32), vec=vec, rowp=rowp, gup=gup, gla_gate_bias=r3(gla_gate_bias),
        gdn_out_norm=r3(gdn_out_norm), gla_out_norm=r3(gla_out_norm), w_out=w_out.astype(BF16),
        xattn_norm=r3(xattn_norm), xattn_w_q=xattn_w_q.astype(BF16), xattn_w_o=xattn_w_o.astype(BF16),
        ffn2_norm=r3(ffn2_norm), ffn2_w_gate=ffn2_w_gate.astype(BF16), ffn2_w_up=ffn2_w_up.astype(BF16),
        ffn2_w_down=ffn2_w_down.astype(BF16), final_norm=final_norm[None, :].astype(F32))


def _trunk(x, mem_k, mem_v, conv_state, gdn_state, gla_state, W):
    b, t, _ = x.shape
    mk = mem_k.reshape(DEPTH, b, N_MEM, D_MODEL)
    mv = mem_v.reshape(DEPTH, b, N_MEM, D_MODEL)
    gla_state = gla_state.reshape(DEPTH, b, GLA_QK, GLA_DV)
    convs, gdns, glas = [], [], []
    for l in range(DEPTH):
        x2 = _ffn(x.reshape(b * t, D_MODEL), l, W['ffn1_norm'], W['ffn1_w_gate'], W['ffn1_w_up'],
                  W['ffn1_w_down'], W['final_norm'], False)
        x, cb, sa, sb = _mix(x2.reshape(b, t, D_MODEL), l, W, conv_state[l], gdn_state[l], gla_state[l])
        x = _xattn(x, l, W['xattn_norm'], W['xattn_w_q'], W['xattn_w_o'], mk, mv)
        x2 = _ffn(x.reshape(b * t, D_MODEL), l, W['ffn2_norm'], W['ffn2_w_gate'], W['ffn2_w_up'],
                  W['ffn2_w_down'], W['final_norm'], l == DEPTH - 1)
        x = x2.reshape(b, t, D_MODEL)
        convs.append(cb)
        gdns.append(sa)
        glas.append(sb.reshape(b, GLA_HEADS, GLA_DK, GLA_DV))
    return x, jnp.stack(convs), jnp.stack(gdns), jnp.stack(glas)


def kernel(x_prompt, x_sample, mem_prompt, cache_mem_k, cache_mem_v, state_gdn_conv, state_gdn, state_gla, ffn1_norm, ffn1_w_gate, ffn1_w_up, ffn1_w_down, mix_norm, w_in, gdn_conv_w, gdn_a_log, gdn_dt_bias, gdn_out_norm, gla_gate_up, gla_gate_bias, gla_out_norm, w_out, xattn_norm, xattn_w_q, xattn_w_k, xattn_w_v, xattn_w_o, ffn2_norm, ffn2_w_gate, ffn2_w_up, ffn2_w_down, final_norm):
    W = _prep_weights(ffn1_norm, ffn1_w_gate, ffn1_w_up, ffn1_w_down, mix_norm, w_in, gdn_conv_w, gdn_a_log,
                      gdn_dt_bias, gdn_out_norm, gla_gate_up, gla_gate_bias, gla_out_norm, w_out, xattn_norm,
                      xattn_w_q, xattn_w_o, ffn2_norm, ffn2_w_gate, ffn2_w_up, ffn2_w_down, final_norm)
    bp = x_prompt.shape[0]
    mem2 = mem_prompt.reshape(bp * N_MEM, D_MODEL).astype(BF16)
    mk_p, mv_p = _memkv(mem2, xattn_w_k.astype(BF16), xattn_w_v.astype(BF16))
    mem_k_p = mk_p.reshape(DEPTH, bp, N_MEM, XA_HEADS, XA_DH)
    mem_v_p = mv_p.reshape(DEPTH, bp, N_MEM, XA_HEADS, XA_DH)
    conv0 = jnp.zeros((DEPTH, bp, CONV_W - 1, GDN_CONV_CH), F32)
    gdn0 = jnp.zeros((DEPTH, bp, GDN_HEADS, GDN_DK, GDN_DV), F32)
    gla0 = jnp.zeros((DEPTH, bp, GLA_HEADS, GLA_DK, GLA_DV), F32)
    y_p, conv_p, gdn_p, gla_p = _trunk(x_prompt, mem_k_p, mem_v_p, conv0, gdn0, gla0, W)
    y_s, conv_s, gdn_s, gla_s = _trunk(x_sample, cache_mem_k, cache_mem_v, state_gdn_conv, state_gdn,
                                       state_gla, W)
    return (y_p, y_s, conv_p, gdn_p, gla_p, mem_k_p, mem_v_p, conv_s, gdn_s, gla_s)
```

```python
import functools
import math

import numpy as np
import jax
import jax.numpy as jnp
from jax import lax
from jax.experimental import pallas as pl
from jax.experimental.pallas import tpu as pltpu

F32 = jnp.float32
BF16 = jnp.bfloat16

D_MODEL = 1024
DEPTH = 4
EPS = 1e-6
N_MEM = 256
CONV_W = 4
GDN_HEADS, GDN_DK, GDN_DV = 4, 128, 128
GDN_QK = GDN_HEADS * GDN_DK
GDN_V = GDN_HEADS * GDN_DV
GDN_CONV_CH = 2 * GDN_QK + GDN_V
GLA_HEADS, GLA_DK, GLA_DV = 4, 64, 128
GLA_RANK = 16
GLA_TAU = 16.0
GLA_QK = GLA_HEADS * GLA_DK
GLA_V = GLA_HEADS * GLA_DV
MIX_W = GDN_V + GLA_V
XA_HEADS = 4
XA_DH = D_MODEL // XA_HEADS
D_FF = 2816
IN_SIZES = (GDN_CONV_CH, GDN_HEADS, GDN_HEADS, GDN_V, GLA_QK, GLA_QK, GLA_V, GLA_V, GLA_RANK)
IN_OFFSETS = tuple(int(o) for o in np.cumsum((0,) + IN_SIZES)[:-1])

LANES = 128
SUBLANES = 8
C_QKV = 0
C_Z = C_QKV + GDN_CONV_CH
C_GQ = C_Z + GDN_V
C_GK = C_GQ + GLA_QK
C_GV = C_GK + GLA_QK
C_GR = C_GV + GLA_V
C_SM = C_GR + GLA_V
IN_COLS_PAD = C_SM + LANES
SM_B, SM_A, SM_R = 0, GDN_HEADS, 2 * GDN_HEADS

VMEM_LIMIT = 56 * 1024 * 1024
MIX_TILE = 256
MIX_ROWS = 2
ROW_TILE = 512
FFN_HALVES = 2
FF_CHUNK = 256


def _dot(a, b):
    return jnp.dot(a, b, preferred_element_type=F32)


def _dot_nt(a, b):
    return lax.dot_general(a, b, (((1,), (1,)), ((), ())), preferred_element_type=F32)


def _dot_tn(a, b):
    return lax.dot_general(a, b, (((0,), (0,)), ((), ())), preferred_element_type=F32)


def _rms(x, g):
    return x * lax.rsqrt(jnp.mean(x * x, axis=-1, keepdims=True) + EPS) * g


def _silu(x):
    return x * jax.nn.sigmoid(x)


def _softplus(x):
    return jnp.maximum(x, 0.0) + jnp.log(1.0 + jnp.exp(-jnp.abs(x)))


def _split2(x):
    hi = x.astype(BF16)
    lo = (x - hi.astype(F32)).astype(BF16)
    return hi, lo


def _ffn_body(x_ref, g_ref, wg_ref, wu_ref, wd_ref, fg_ref, o_ref, a_ref, *, final, halves):
    tm = x_ref.shape[0] // halves
    hs = {}

    def norm(i):
        hs[i] = _rms(x_ref[i * tm:(i + 1) * tm, :], g_ref[...]).astype(BF16)

    norm(0)
    for i in range(halves):
        rows = slice(i * tm, (i + 1) * tm)
        for k, f in enumerate(range(0, D_FF, FF_CHUNK)):
            gt = _dot(hs[i], wg_ref[:, f:f + FF_CHUNK])
            up = _dot(hs[i], wu_ref[:, f:f + FF_CHUNK])
            a_ref[rows, f:f + FF_CHUNK] = (_silu(gt) * up).astype(BF16)
            if k == 0 and i + 1 < halves:
                norm(i + 1)
        y = x_ref[rows, :] + 0.5 * _dot(a_ref[rows, :], wd_ref[...])
        if final:
            y = _rms(y, fg_ref[...])
        o_ref[rows, :] = y


def _ffn(x2, l, gain, wg, wu, wd, fgain, final):
    m = x2.shape[0]
    tm = min(m, FFN_HALVES * ROW_TILE)
    halves = FFN_HALVES if tm == FFN_HALVES * ROW_TILE else 1
    wspec = lambda shp: pl.BlockSpec((None,) + shp, lambda i: (l, 0, 0), pipeline_mode=pl.Buffered(1))
    return pl.pallas_call(
        functools.partial(_ffn_body, final=final, halves=halves),
        grid=(m // tm,),
        in_specs=[
            pl.BlockSpec((tm, D_MODEL), lambda i: (i, 0)),
            wspec((1, D_MODEL)),
            wspec((D_MODEL, D_FF)),
            wspec((D_MODEL, D_FF)),
            wspec((D_FF, D_MODEL)),
            pl.BlockSpec((1, D_MODEL), lambda i: (0, 0)),
        ],
        out_specs=pl.BlockSpec((tm, D_MODEL), lambda i: (i, 0)),
        out_shape=jax.ShapeDtypeStruct((m, D_MODEL), F32),
        scratch_shapes=[pltpu.VMEM((tm, D_FF), BF16)],
        compiler_params=pltpu.CompilerParams(
            dimension_semantics=("arbitrary",), vmem_limit_bytes=VMEM_LIMIT),
        name="ffn",
    )(x2, gain, wg, wu, wd, fgain)


def _memkv_body(m_ref, wk_ref, wv_ref, k_ref, v_ref):
    m = m_ref[...]
    k_ref[...] = _dot(m, wk_ref[...])
    v_ref[...] = _dot(m, wv_ref[...])


def _memkv(mem2, wk, wv):
    m = mem2.shape[0]
    tm = min(m, 1024)
    return pl.pallas_call(
        _memkv_body,
        grid=(DEPTH, m // tm),
        in_specs=[
            pl.BlockSpec((tm, D_MODEL), lambda l, i: (i, 0)),
            pl.BlockSpec((None, D_MODEL, D_MODEL), lambda l, i: (l, 0, 0)),
            pl.BlockSpec((None, D_MODEL, D_MODEL), lambda l, i: (l, 0, 0)),
        ],
        out_specs=[
            pl.BlockSpec((None, tm, D_MODEL), lambda l, i: (l, i, 0)),
            pl.BlockSpec((None, tm, D_MODEL), lambda l, i: (l, i, 0)),
        ],
        out_shape=[jax.ShapeDtypeStruct((DEPTH, m, D_MODEL), F32)] * 2,
        compiler_params=pltpu.CompilerParams(
            dimension_semantics=("arbitrary", "arbitrary"), vmem_limit_bytes=VMEM_LIMIT),
        name="memkv",
    )(mem2, wk, wv)


def _xattn_body(x_ref, g_ref, wq_ref, wo_ref, mk_ref, mv_ref, o_ref):
    x = x_ref[...]
    h = _rms(x, g_ref[...]).astype(BF16)
    mk = mk_ref[...].astype(BF16)
    mv = mv_ref[...].astype(BF16)
    sls = [slice(hh * XA_DH, (hh + 1) * XA_DH) for hh in range(XA_HEADS)]
    qs = [_dot(h, wq_ref[:, sl]).astype(BF16) for sl in sls]
    ss = [_dot_nt(q, mk[:, sl]) * (XA_DH ** -0.5) for q, sl in zip(qs, sls)]
    es = [jnp.exp(s - jnp.max(s, axis=-1, keepdims=True)) for s in ss]
    prs = [(e / jnp.sum(e, axis=-1, keepdims=True)).astype(BF16) for e in es]
    o = jnp.concatenate([_dot(pr, mv[:, sl]) for pr, sl in zip(prs, sls)], axis=1).astype(BF16)
    o_ref[...] = x + _dot(o, wo_ref[...])


def _xattn(x, l, gain, wq, wo, mk, mv):
    b, t, _ = x.shape
    tt = min(t, ROW_TILE)
    wspec = lambda shp: pl.BlockSpec((None,) + shp, lambda i, j: (l, 0, 0))
    return pl.pallas_call(
        _xattn_body,
        grid=(b, t // tt),
        in_specs=[
            pl.BlockSpec((None, tt, D_MODEL), lambda i, j: (i, j, 0)),
            wspec((1, D_MODEL)),
            wspec((D_MODEL, D_MODEL)),
            wspec((D_MODEL, D_MODEL)),
            pl.BlockSpec((None, None, N_MEM, D_MODEL), lambda i, j: (l, i, 0, 0)),
            pl.BlockSpec((None, None, N_MEM, D_MODEL), lambda i, j: (l, i, 0, 0)),
        ],
        out_specs=pl.BlockSpec((None, tt, D_MODEL), lambda i, j: (i, j, 0)),
        out_shape=jax.ShapeDtypeStruct(x.shape, F32),
        compiler_params=pltpu.CompilerParams(
            dimension_semantics=("arbitrary", "arbitrary"), vmem_limit_bytes=VMEM_LIMIT),
        name="xattn",
    )(x, gain, wq, wo, mk, mv)


def _prefix_matrix(c):
    nlev = int(math.log2(c))
    assert 1 << nlev == c
    tri = np.tril(np.ones((c, c), np.float32))
    return jnp.asarray(tri, dtype=BF16), nlev


def _upper_rows(x, b):
    return jnp.concatenate([x[s + b:s + 2 * b] for s in range(0, x.shape[0], 2 * b)], axis=0)


def _merge_rows(x, up, b):
    pieces = []
    for k, s in enumerate(range(0, x.shape[0], 2 * b)):
        pieces += [x[s:s + b], up[k * b:(k + 1) * b]]
    return jnp.concatenate(pieces, axis=0)


def _load_upper(ref, b):
    return jnp.concatenate([ref[s + b:s + 2 * b, :] for s in range(0, ref.shape[0], 2 * b)], axis=0)


def _store_upper(ref, up, b):
    for k, s in enumerate(range(0, ref.shape[0], 2 * b)):
        ref[s + b:s + 2 * b, :] = up[k * b:(k + 1) * b]


def _alternate(*gens):
    live = list(gens)
    while live:
        for g in list(live):
            if next(g, StopIteration) is StopIteration:
                live.remove(g)


def _mix_body(x_ref, g_ref, win_ref, wst_ref, cw_ref, vec_ref, rowp_ref, gup_ref, gb_ref, gng_ref,
              gnl_ref, wout_ref, cs_ref, conv0_ref, sg0_ref, sl0_ref,
              y_ref, convo_ref, sgo_ref, slo_ref,
              cbuf, sgdn, sgla, a_s, tinv_s, pm_s, *, tt, nlev, rows):
    t = pl.program_id(1)
    nt = pl.num_programs(1)
    tail = SUBLANES - (CONV_W - 1)

    @pl.when(t == 0)
    def _():
        cbuf[...] = jnp.zeros(cbuf.shape, F32)
        cbuf[:, tail:SUBLANES, :] = conv0_ref[...]
        sgdn[...] = sg0_ref[...]
        sgla[...] = sl0_ref[...]

    ii = lax.broadcasted_iota(jnp.int32, (tt, tt), 0)
    jj = lax.broadcasted_iota(jnp.int32, (tt, tt), 1)
    lower = ii > jj
    lv = jnp.where(lower, 31 - lax.clz(ii ^ jj), -1)
    eye = jnp.where(ii == jj, 1.0, 0.0).astype(F32)
    ii4 = jnp.concatenate([ii] * GLA_HEADS, axis=0)
    jj4 = jnp.concatenate([jj] * GLA_HEADS, axis=0)
    lv4 = jnp.concatenate([lv] * GLA_HEADS, axis=0)
    row_id = lax.broadcasted_iota(jnp.int32, (tt, GLA_QK), 0)
    lane_head = lax.broadcasted_iota(jnp.int32, (1, GLA_QK), 1) // GLA_DK
    head_mask = [jnp.where(lane_head == hd, 1.0, 0.0).astype(BF16) for hd in range(GLA_HEADS)]
    nq = GDN_QK // LANES

    def stack_heads(v):
        vb = v.astype(BF16)
        return jnp.concatenate([vb * m for m in head_mask], axis=0)

    def project(r, c):
        x = x_ref[r]
        h = _rms(x, g_ref[...]).astype(BF16)
        proj = lambda c0, c1: _dot(h, win_ref[:, c0:c1])
        c['x'] = x
        c['sm'] = proj(C_SM, IN_COLS_PAD)
        c['st'] = _dot_nt(wst_ref[...], h)
        yield
        c['p_qk2'] = proj(C_GQ, C_GV)
        yield
        c['proj'] = proj

    def front(r, c):
        sm, st, p_qk2 = c['sm'], c['st'], c['p_qk2']

        cw = cw_ref[...]
        blocks, tails = [], []
        for cc in range(0, GDN_CONV_CH, LANES):
            cs_ = slice(cc, cc + LANES)
            if cc % (2 * LANES) == 0:
                pair = c['proj'](C_QKV + cc, C_QKV + cc + 2 * LANES)
            seg = pair[:, cc % (2 * LANES):cc % (2 * LANES) + LANES]
            ext = jnp.concatenate([cbuf[r, :, cs_], seg], axis=0)
            conv = ext[tail:tail + tt] * cw[0:1, cs_]
            for i in range(1, CONV_W):
                conv = conv + ext[tail + i:tail + i + tt] * cw[i:i + 1, cs_]
            blocks.append(_silu(conv))
            tails.append(seg[tt - SUBLANES:tt])
            if len(blocks) % 3 == 0:
                yield
        last = jnp.concatenate(tails, axis=1)
        cbuf[r] = last
        c['new_tail'] = last[tail:SUBLANES]
        c['v2'] = c['proj'](C_GV, C_GR).astype(BF16)

        lfull = cs_ref[...]
        vec = vec_ref[...]
        beta_c = jax.nn.sigmoid(sm)
        g_c = vec[0:1, :] * _softplus(sm + vec[1:2, :])
        g_hi, g_lo = _split2(g_c)
        gcum_c = _dot(lfull, g_hi) + _dot(lfull, g_lo)
        rowp = rowp_ref[...]
        g_r = rowp[0] * _softplus(st + rowp[1])
        r_hi, r_lo = _split2(g_r)
        gcum_r = _dot_nt(r_hi, lfull) + _dot_nt(r_lo, lfull)

        c['gate_z'] = _silu(c['proj'](C_Z, C_GQ))
        yield

        xg = _dot(sm.astype(BF16), gup_ref[...]) + gb_ref[...]
        c['la'] = -_softplus(-xg) * (1.0 / GLA_TAU)
        c['q2'] = p_qk2[:, :GLA_QK] * (GLA_DK ** -0.5)
        c['k2'] = p_qk2[:, GLA_QK:]
        c['gate_r'] = _silu(c['proj'](C_GR, C_SM))
        yield

        hs = []
        for hd in range(GDN_HEADS):
            qh, kh, vh = blocks[hd], blocks[nq + hd], blocks[2 * nq + hd]
            qh = qh * lax.rsqrt(jnp.sum(qh * qh, axis=-1, keepdims=True) + EPS) * (GDN_DK ** -0.5)
            kh = kh * lax.rsqrt(jnp.sum(kh * kh, axis=-1, keepdims=True) + EPS)
            bcol = beta_c[:, SM_B + hd:SM_B + hd + 1]
            gc = gcum_c[:, SM_A + hd:SM_A + hd + 1]
            gr = gcum_r[SM_A + hd:SM_A + hd + 1, :]
            dec = jnp.exp(jnp.minimum(gc - gr, 0.0))
            kb = kh * bcol
            kk = _dot_nt(jnp.concatenate([kb, qh], axis=0).astype(BF16), kh.astype(BF16))
            a = jnp.where(lower, kk[:tt] * dec, 0.0)
            qk = jnp.where(ii >= jj, kk[tt:] * dec, 0.0).astype(BF16)
            eg = jnp.exp(gc)
            glast = gr[:, tt - 1:tt]
            a_s[r, hd] = a
            tinv_s[r, hd] = eye - jnp.where(lv == 0, a, 0.0)
            hs.append(dict(
                qk=qk,
                rhs=jnp.concatenate([vh * bcol, kb * eg], axis=1).astype(BF16),
                qg=(qh * eg).astype(BF16), kg=(kh * jnp.exp(glast - gc)).astype(BF16),
                eglast=jnp.exp(glast)))
            yield
        c['hs'] = hs

    def back(r, c):
        hs, q2, k2, v2, la = c['hs'], c['q2'], c['k2'], c['v2'], c['la']
        pm = pm_s.at[r]
        pm[...] = jnp.where(ii4 == jj4, _dot_nt(stack_heads(q2), k2.astype(BF16)), 0.0)
        rq, tot = la, la
        for l in range(nlev):
            b = 1 << l
            sliced = b % SUBLANES == 0
            if l >= 1:
                ys, tbs = [], []
                for hd in range(GDN_HEADS):
                    tb = tinv_s[r, hd].astype(BF16)
                    tbs.append(tb)
                    if sliced:
                        am = jnp.where(_upper_rows(lv, b) == l, _load_upper(a_s.at[r, hd], b), 0.0)
                        y = _dot(am.astype(BF16), tb).astype(BF16)
                        ys.append(_merge_rows(jnp.zeros((tt, tt), BF16), y, b))
                    else:
                        am = jnp.where(lv == l, a_s[r, hd], 0.0)
                        ys.append(_dot(am.astype(BF16), tb).astype(BF16))
                for hd in range(GDN_HEADS):
                    if sliced:
                        tu = _load_upper(tinv_s.at[r, hd], b)
                        _store_upper(tinv_s.at[r, hd], tu - _dot(tu.astype(BF16), ys[hd]), b)
                    else:
                        tinv_s[r, hd] = tinv_s[r, hd] - _dot(tbs[hd], ys[hd])
            qt = q2 * jnp.exp(rq)
            kt = (k2 * jnp.exp(tot - rq)).astype(BF16)
            if sliced:
                rr = _dot_nt(stack_heads(_upper_rows(qt, b)), kt)
                _store_upper(pm, jnp.where(_upper_rows(lv4, b) == l, rr, _load_upper(pm, b)), b)
            else:
                pm[...] = jnp.where(lv4 == l, _dot_nt(stack_heads(qt), kt), pm[...])
            prev = pltpu.roll(tot, b, axis=0)
            nxt = pltpu.roll(tot, tt - b, axis=0)
            hi_half = (row_id & b) != 0
            rq = rq + jnp.where(hi_half, prev, 0.0)
            tot = tot + jnp.where(hi_half, prev, nxt)
            yield
        bcum, brev = rq, tot - rq

        outs = []
        uws = [_dot(tinv_s[r, hd].astype(BF16), hs[hd]['rhs']) for hd in range(GDN_HEADS)]
        s_olds = [sgdn[r, hd] for hd in range(GDN_HEADS)]
        wqs = []
        zs = jnp.zeros((GDN_DK, GDN_DV), BF16)
        for ha in range(0, GDN_HEADS, 2):
            hb = ha + 1
            lhs = jnp.concatenate([
                jnp.concatenate([uws[ha][:, GDN_DV:], uws[hb][:, GDN_DV:]], axis=1).astype(BF16),
                jnp.concatenate([hs[ha]['qg'], hs[hb]['qg']], axis=1)], axis=0)
            sbd = jnp.concatenate([
                jnp.concatenate([s_olds[ha].astype(BF16), zs], axis=1),
                jnp.concatenate([zs, s_olds[hb].astype(BF16)], axis=1)], axis=0)
            res = _dot(lhs, sbd)
            wqs += [res[:, :GDN_DV], res[:, GDN_DV:]]
        unews = [(uw[:, :GDN_DV] - wq[:tt]).astype(BF16) for uw, wq in zip(uws, wqs)]
        yield
        for hd in range(GDN_HEADS):
            d = hs[hd]
            o = wqs[hd][tt:] + _dot(d['qk'], unews[hd])
            sgdn[r, hd] = s_olds[hd] * d['eglast'] + _dot_tn(d['kg'], unews[hd])
            on = _rms(o, gng_ref[...])
            outs.append(on * c['gate_z'][:, hd * GDN_DV:(hd + 1) * GDN_DV])
        yield

        s2 = sgla[r]
        row_head = lax.broadcasted_iota(jnp.int32, (GLA_QK, 1), 0) // GLA_DK
        s_bd = jnp.concatenate([jnp.where(row_head == hd, s2, 0.0) for hd in range(GLA_HEADS)],
                               axis=1).astype(BF16)
        inter = _dot((q2 * jnp.exp(bcum)).astype(BF16), s_bd)
        ds = _dot_tn((k2 * jnp.exp(brev)).astype(BF16), v2)
        la_hi, la_lo = _split2(la)
        ones = jnp.ones((tt, GLA_DV), BF16)
        blast = _dot_tn(la_hi, ones) + _dot_tn(la_lo, ones)
        yield
        for hd in range(GLA_HEADS):
            vs = slice(hd * GLA_DV, (hd + 1) * GLA_DV)
            rs = slice(hd * GLA_DK, (hd + 1) * GLA_DK)
            o = _dot(pm[hd * tt:(hd + 1) * tt, :].astype(BF16), v2[:, vs]) + inter[:, vs]
            sgla[r, rs, :] = s2[rs, :] * jnp.exp(blast[rs, :]) + ds[rs, vs]
            on = _rms(o, gnl_ref[...])
            outs.append(on * c['gate_r'][:, hd * GLA_DV:(hd + 1) * GLA_DV])
        yield

        o_all = jnp.concatenate(outs, axis=1).astype(BF16)
        y_ref[r] = c['x'] + _dot(o_all, wout_ref[...])
        yield

    ctx = [dict() for _ in range(rows)]
    stages = [g(r, ctx[r]) for r in range(rows) for g in (project, front, back)]
    order = [[stages[0]]]
    for r in range(rows):
        if r + 1 < rows:
            order.append([stages[3 * r + 1], stages[3 * r + 3]])
        else:
            order.append([stages[3 * r + 1]])
        if r >= 1:
            order[-1].insert(0, stages[3 * r - 1])
    order.append([stages[3 * rows - 1]])
    for group in order:
        if len(group) == 1:
            for _ in group[0]:
                pass
        else:
            _alternate(*group)

    @pl.when(t == nt - 1)
    def _():
        for r in range(rows):
            convo_ref[r] = ctx[r]['new_tail']
        sgo_ref[...] = sgdn[...]
        slo_ref[...] = sgla[...]


def _mix(x, l, W, conv0, sg0, sl0):
    b, t, _ = x.shape
    tt = min(t, MIX_TILE)
    nt = t // tt
    rows = MIX_ROWS if b % MIX_ROWS == 0 else 1
    cs, nlev = _prefix_matrix(tt)
    rowp = jnp.broadcast_to(W['rowp'][l][:, :, None], (2, 2 * SUBLANES, tt))
    wspec = lambda shp: pl.BlockSpec((None,) + shp, lambda i, j: (l,) + (0,) * len(shp))
    full = lambda shp: pl.BlockSpec(shp, lambda i, j: (0,) * len(shp))
    sspec = lambda shp: pl.BlockSpec((rows,) + shp, lambda i, j: (i,) + (0,) * len(shp))
    return pl.pallas_call(
        functools.partial(_mix_body, tt=tt, nlev=nlev, rows=rows),
        grid=(b // rows, nt),
        in_specs=[
            pl.BlockSpec((rows, tt, D_MODEL), lambda i, j: (i, j, 0)),
            wspec((1, D_MODEL)),
            wspec((D_MODEL, IN_COLS_PAD)),
            wspec((2 * SUBLANES, D_MODEL)),
            wspec((CONV_W, GDN_CONV_CH)),
            wspec((SUBLANES, LANES)),
            full((2, 2 * SUBLANES, tt)),
            wspec((LANES, GLA_QK)),
            wspec((1, GLA_QK)),
            wspec((1, GDN_DV)),
            wspec((1, GLA_DV)),
            wspec((MIX_W, D_MODEL)),
            full((tt, tt)),
            sspec((CONV_W - 1, GDN_CONV_CH)),
            sspec((GDN_HEADS, GDN_DK, GDN_DV)),
            sspec((GLA_QK, GLA_DV)),
        ],
        out_specs=[
            pl.BlockSpec((rows, tt, D_MODEL), lambda i, j: (i, j, 0)),
            sspec((CONV_W - 1, GDN_CONV_CH)),
            sspec((GDN_HEADS, GDN_DK, GDN_DV)),
            sspec((GLA_QK, GLA_DV)),
        ],
        out_shape=[
            jax.ShapeDtypeStruct(x.shape, F32),
            jax.ShapeDtypeStruct((b, CONV_W - 1, GDN_CONV_CH), F32),
            jax.ShapeDtypeStruct((b, GDN_HEADS, GDN_DK, GDN_DV), F32),
            jax.ShapeDtypeStruct((b, GLA_QK, GLA_DV), F32),
        ],
        scratch_shapes=[
            pltpu.VMEM((rows, SUBLANES, GDN_CONV_CH), F32),
            pltpu.VMEM((rows, GDN_HEADS, GDN_DK, GDN_DV), F32),
            pltpu.VMEM((rows, GLA_QK, GLA_DV), F32),
            pltpu.VMEM((rows, GDN_HEADS, tt, tt), F32),
            pltpu.VMEM((rows, GDN_HEADS, tt, tt), F32),
            pltpu.VMEM((rows, GLA_HEADS * tt, tt), F32),
        ],
        compiler_params=pltpu.CompilerParams(
            dimension_semantics=("arbitrary", "arbitrary"), vmem_limit_bytes=VMEM_LIMIT),
        name="mix",
    )(x, W['mix_norm'], W['w_in'], W['w_st'], W['gdn_conv_w'], W['vec'], rowp, W['gup'], W['gla_gate_bias'],
      W['gdn_out_norm'], W['gla_out_norm'], W['w_out'], cs, conv0, sg0, sl0)


def _prep_weights(ffn1_norm, ffn1_w_gate, ffn1_w_up, ffn1_w_down, mix_norm, w_in, gdn_conv_w, gdn_a_log,
                  gdn_dt_bias, gdn_out_norm, gla_gate_up, gla_gate_bias, gla_out_norm, w_out, xattn_norm,
                  xattn_w_q, xattn_w_o, ffn2_norm, ffn2_w_gate, ffn2_w_up, ffn2_w_down, final_norm):
    o = IN_OFFSETS
    seg = lambda k: w_in[:, :, o[k]:o[k] + IN_SIZES[k]]
    small = jnp.concatenate([seg(1), seg(2), seg(8)], axis=-1)
    small = jnp.pad(small, ((0, 0), (0, 0), (0, LANES - small.shape[-1])))
    w_in_r = jnp.concatenate([seg(0), seg(3), seg(4), seg(5), seg(6), seg(7), small], axis=-1).astype(BF16)
    w_st = jnp.concatenate([seg(1), seg(2)], axis=-1)
    w_st = jnp.pad(jnp.swapaxes(w_st, 1, 2), ((0, 0), (0, SUBLANES), (0, 0))).astype(BF16)
    nega = -jnp.exp(gdn_a_log.astype(F32))
    dtb = gdn_dt_bias.astype(F32)
    lane_pad = lambda v: jnp.pad(v, ((0, 0), (SM_A, LANES - SM_A - GDN_HEADS)))
    vec = jnp.stack([lane_pad(nega), lane_pad(dtb)], axis=1)
    vec = jnp.pad(vec, ((0, 0), (0, SUBLANES - 2), (0, 0)))
    row_pad = lambda v: jnp.pad(v, ((0, 0), (SM_A, 2 * SUBLANES - SM_A - GDN_HEADS)))
    rowp = jnp.stack([row_pad(nega), row_pad(dtb)], axis=1)
    gup = jnp.pad(gla_gate_up, ((0, 0), (SM_R, LANES - SM_R - GLA_RANK), (0, 0))).astype(BF16)
    r3 = lambda v: v[:, None, :].astype(F32)
    return dict(
        ffn1_norm=r3(ffn1_norm), ffn1_w_gate=ffn1_w_gate.astype(BF16), ffn1_w_up=ffn1_w_up.astype(BF16),
        ffn1_w_down=ffn1_w_down.astype(BF16), mix_norm=r3(mix_norm), w_in=w_in_r, w_st=w_st,
        gdn_conv_w=gdn_conv_w.astype(F32), vec=vec, rowp=rowp, gup=gup, gla_gate_bias=r3(gla_gate_bias),
        gdn_out_norm=r3(gdn_out_norm), gla_out_norm=r3(gla_out_norm), w_out=w_out.astype(BF16),
        xattn_norm=r3(xattn_norm), xattn_w_q=xattn_w_q.astype(BF16), xattn_w_o=xattn_w_o.astype(BF16),
        ffn2_norm=r3(ffn2_norm), ffn2_w_gate=ffn2_w_gate.astype(BF16), ffn2_w_up=ffn2_w_up.astype(BF16),
        ffn2_w_down=ffn2_w_down.astype(BF16), final_norm=final_norm[None, :].astype(F32))


def _trunk(x, mem_k, mem_v, conv_state, gdn_state, gla_state, W):
    b, t, _ = x.shape
    mk = mem_k.reshape(DEPTH, b, N_MEM, D_MODEL)
    mv = mem_v.reshape(DEPTH, b, N_MEM, D_MODEL)
    gla_state = gla_state.reshape(DEPTH, b, GLA_QK, GLA_DV)
    convs, gdns, glas = [], [], []
    for l in range(DEPTH):
        x2 = _ffn(x.reshape(b * t, D_MODEL), l, W['ffn1_norm'], W['ffn1_w_gate'], W['ffn1_w_up'],
                  W['ffn1_w_down'], W['final_norm'], False)
        x, cb, sa, sb = _mix(x2.reshape(b, t, D_MODEL), l, W, conv_state[l], gdn_state[l], gla_state[l])
        x = _xattn(x, l, W['xattn_norm'], W['xattn_w_q'], W['xattn_w_o'], mk, mv)
        x2 = _ffn(x.reshape(b * t, D_MODEL), l, W['ffn2_norm'], W['ffn2_w_gate'], W['ffn2_w_up'],
                  W['ffn2_w_down'], W['final_norm'], l == DEPTH - 1)
        x = x2.reshape(b, t, D_MODEL)
        convs.append(cb)
        gdns.append(sa)
        glas.append(sb.reshape(b, GLA_HEADS, GLA_DK, GLA_DV))
    return x, jnp.stack(convs), jnp.stack(gdns), jnp.stack(glas)


def kernel(x_prompt, x_sample, mem_prompt, cache_mem_k, cache_mem_v, state_gdn_conv, state_gdn, state_gla, ffn1_norm, ffn1_w_gate, ffn1_w_up, ffn1_w_down, mix_norm, w_in, gdn_conv_w, gdn_a_log, gdn_dt_bias, gdn_out_norm, gla_gate_up, gla_gate_bias, gla_out_norm, w_out, xattn_norm, xattn_w_q, xattn_w_k, xattn_w_v, xattn_w_o, ffn2_norm, ffn2_w_gate, ffn2_w_up, ffn2_w_down, final_norm):
    W = _prep_weights(ffn1_norm, ffn1_w_gate, ffn1_w_up, ffn1_w_down, mix_norm, w_in, gdn_conv_w, gdn_a_log,
                      gdn_dt_bias, gdn_out_norm, gla_gate_up, gla_gate_bias, gla_out_norm, w_out, xattn_norm,
                      xattn_w_q, xattn_w_o, ffn2_norm, ffn2_w_gate, ffn2_w_up, ffn2_w_down, final_norm)
    bp = x_prompt.shape[0]
    mem2 = mem_prompt.reshape(bp * N_MEM, D_MODEL).astype(BF16)
    mk_p, mv_p = _memkv(mem2, xattn_w_k.astype(BF16), xattn_w_v.astype(BF16))
    mem_k_p = mk_p.reshape(DEPTH, bp, N_MEM, XA_HEADS, XA_DH)
    mem_v_p = mv_p.reshape(DEPTH, bp, N_MEM, XA_HEADS, XA_DH)
    conv0 = jnp.zeros((DEPTH, bp, CONV_W - 1, GDN_CONV_CH), F32)
    gdn0 = jnp.zeros((DEPTH, bp, GDN_HEADS, GDN_DK, GDN_DV), F32)
    gla0 = jnp.zeros((DEPTH, bp, GLA_HEADS, GLA_DK, GLA_DV), F32)
    y_p, conv_p, gdn_p, gla_p = _trunk(x_prompt, mem_k_p, mem_v_p, conv0, gdn0, gla0, W)
    y_s, conv_s, gdn_s, gla_s = _trunk(x_sample, cache_mem_k, cache_mem_v, state_gdn_conv, state_gdn,
                                       state_gla, W)
    return (y_p, y_s, conv_p, gdn_p, gla_p, mem_k_p, mem_v_p, conv_s, gdn_s, gla_s)
```

```python
import functools
import math

import numpy as np
import jax
import jax.numpy as jnp
from jax import lax
from jax.experimental import pallas as pl
from jax.experimental.pallas import tpu as pltpu

F32 = jnp.float32
BF16 = jnp.bfloat16

D_MODEL = 1024
DEPTH = 4
EPS = 1e-6
N_MEM = 256
CONV_W = 4
GDN_HEADS, GDN_DK, GDN_DV = 4, 128, 128
GDN_QK = GDN_HEADS * GDN_DK
GDN_V = GDN_HEADS * GDN_DV
GDN_CONV_CH = 2 * GDN_QK + GDN_V
GLA_HEADS, GLA_DK, GLA_DV = 4, 64, 128
GLA_RANK = 16
GLA_TAU = 16.0
GLA_QK = GLA_HEADS * GLA_DK
GLA_V = GLA_HEADS * GLA_DV
MIX_W = GDN_V + GLA_V
XA_HEADS = 4
XA_DH = D_MODEL // XA_HEADS
D_FF = 2816
IN_SIZES = (GDN_CONV_CH, GDN_HEADS, GDN_HEADS, GDN_V, GLA_QK, GLA_QK, GLA_V, GLA_V, GLA_RANK)
IN_OFFSETS = tuple(int(o) for o in np.cumsum((0,) + IN_SIZES)[:-1])

LANES = 128
SUBLANES = 8
C_QKV = 0
C_Z = C_QKV + GDN_CONV_CH
C_GQ = C_Z + GDN_V
C_GK = C_GQ + GLA_QK
C_GV = C_GK + GLA_QK
C_GR = C_GV + GLA_V
C_SM = C_GR + GLA_V
IN_COLS_PAD = C_SM + LANES
SM_B, SM_A, SM_R = 0, GDN_HEADS, 2 * GDN_HEADS

VMEM_LIMIT = 56 * 1024 * 1024
MIX_TILE = 256
MIX_ROWS = 2
ROW_TILE = 512
XATTN_TILE = 1024
FFN_HALVES = 2
FF_CHUNK = 256


def _dot(a, b):
    return jnp.dot(a, b, preferred_element_type=F32)


def _dot_nt(a, b):
    return lax.dot_general(a, b, (((1,), (1,)), ((), ())), preferred_element_type=F32)


def _dot_tn(a, b):
    return lax.dot_general(a, b, (((0,), (0,)), ((), ())), preferred_element_type=F32)


def _rms(x, g):
    return x * lax.rsqrt(jnp.mean(x * x, axis=-1, keepdims=True) + EPS) * g


def _silu(x):
    return x * jax.nn.sigmoid(x)


def _softplus(x):
    return jnp.maximum(x, 0.0) + jnp.log(1.0 + jnp.exp(-jnp.abs(x)))


def _split2(x):
    hi = x.astype(BF16)
    lo = (x - hi.astype(F32)).astype(BF16)
    return hi, lo


def _ffn_body(x_ref, g_ref, wg_ref, wu_ref, wd_ref, fg_ref, o_ref, a_ref, *, final, halves):
    tm = x_ref.shape[0] // halves
    hs = {}

    def norm(i):
        hs[i] = _rms(x_ref[i * tm:(i + 1) * tm, :], g_ref[...]).astype(BF16)

    norm(0)
    for i in range(halves):
        rows = slice(i * tm, (i + 1) * tm)
        for k, f in enumerate(range(0, D_FF, FF_CHUNK)):
            gt = _dot(hs[i], wg_ref[:, f:f + FF_CHUNK])
            up = _dot(hs[i], wu_ref[:, f:f + FF_CHUNK])
            a_ref[rows, f:f + FF_CHUNK] = (_silu(gt) * up).astype(BF16)
            if k == 0 and i + 1 < halves:
                norm(i + 1)
        y = x_ref[rows, :] + 0.5 * _dot(a_ref[rows, :], wd_ref[...])
        if final:
            y = _rms(y, fg_ref[...])
        o_ref[rows, :] = y


def _ffn(x2, l, gain, wg, wu, wd, fgain, final):
    m = x2.shape[0]
    tm = min(m, FFN_HALVES * ROW_TILE)
    halves = FFN_HALVES if tm == FFN_HALVES * ROW_TILE else 1
    wspec = lambda shp: pl.BlockSpec((None,) + shp, lambda i: (l, 0, 0), pipeline_mode=pl.Buffered(1))
    return pl.pallas_call(
        functools.partial(_ffn_body, final=final, halves=halves),
        grid=(m // tm,),
        in_specs=[
            pl.BlockSpec((tm, D_MODEL), lambda i: (i, 0)),
            wspec((1, D_MODEL)),
            wspec((D_MODEL, D_FF)),
            wspec((D_MODEL, D_FF)),
            wspec((D_FF, D_MODEL)),
            pl.BlockSpec((1, D_MODEL), lambda i: (0, 0)),
        ],
        out_specs=pl.BlockSpec((tm, D_MODEL), lambda i: (i, 0)),
        out_shape=jax.ShapeDtypeStruct((m, D_MODEL), F32),
        scratch_shapes=[pltpu.VMEM((tm, D_FF), BF16)],
        compiler_params=pltpu.CompilerParams(
            dimension_semantics=("arbitrary",), vmem_limit_bytes=VMEM_LIMIT),
        name="ffn",
    )(x2, gain, wg, wu, wd, fgain)


def _memkv_body(m_ref, wk_ref, wv_ref, k_ref, v_ref):
    m = m_ref[...]
    k_ref[...] = _dot(m, wk_ref[...])
    v_ref[...] = _dot(m, wv_ref[...])


def _memkv(mem2, wk, wv):
    m = mem2.shape[0]
    tm = min(m, 1024)
    return pl.pallas_call(
        _memkv_body,
        grid=(DEPTH, m // tm),
        in_specs=[
            pl.BlockSpec((tm, D_MODEL), lambda l, i: (i, 0)),
            pl.BlockSpec((None, D_MODEL, D_MODEL), lambda l, i: (l, 0, 0)),
            pl.BlockSpec((None, D_MODEL, D_MODEL), lambda l, i: (l, 0, 0)),
        ],
        out_specs=[
            pl.BlockSpec((None, tm, D_MODEL), lambda l, i: (l, i, 0)),
            pl.BlockSpec((None, tm, D_MODEL), lambda l, i: (l, i, 0)),
        ],
        out_shape=[jax.ShapeDtypeStruct((DEPTH, m, D_MODEL), F32)] * 2,
        compiler_params=pltpu.CompilerParams(
            dimension_semantics=("arbitrary", "arbitrary"), vmem_limit_bytes=VMEM_LIMIT),
        name="memkv",
    )(mem2, wk, wv)


def _xattn_body(x_ref, g_ref, wq_ref, wo_ref, mk_ref, mv_ref, o_ref):
    x = x_ref[...]
    h = _rms(x, g_ref[...]).astype(BF16)
    mk = mk_ref[...].astype(BF16)
    mv = mv_ref[...].astype(BF16)
    sls = [slice(hh * XA_DH, (hh + 1) * XA_DH) for hh in range(XA_HEADS)]
    qs = [_dot(h, wq_ref[:, sl]).astype(BF16) for sl in sls]
    ss = [_dot_nt(q, mk[:, sl]) * (XA_DH ** -0.5) for q, sl in zip(qs, sls)]
    es = [jnp.exp(s - jnp.max(s, axis=-1, keepdims=True)) for s in ss]
    prs = [(e / jnp.sum(e, axis=-1, keepdims=True)).astype(BF16) for e in es]
    o = jnp.concatenate([_dot(pr, mv[:, sl]) for pr, sl in zip(prs, sls)], axis=1).astype(BF16)
    o_ref[...] = x + _dot(o, wo_ref[...])


def _xattn(x, l, gain, wq, wo, mk, mv):
    b, t, _ = x.shape
    tt = min(t, XATTN_TILE)
    wspec = lambda shp: pl.BlockSpec((None,) + shp, lambda i, j: (l, 0, 0))
    return pl.pallas_call(
        _xattn_body,
        grid=(b, t // tt),
        in_specs=[
            pl.BlockSpec((None, tt, D_MODEL), lambda i, j: (i, j, 0)),
            wspec((1, D_MODEL)),
            wspec((D_MODEL, D_MODEL)),
            wspec((D_MODEL, D_MODEL)),
            pl.BlockSpec((None, None, N_MEM, D_MODEL), lambda i, j: (l, i, 0, 0)),
            pl.BlockSpec((None, None, N_MEM, D_MODEL), lambda i, j: (l, i, 0, 0)),
        ],
        out_specs=pl.BlockSpec((None, tt, D_MODEL), lambda i, j: (i, j, 0)),
        out_shape=jax.ShapeDtypeStruct(x.shape, F32),
        compiler_params=pltpu.CompilerParams(
            dimension_semantics=("arbitrary", "arbitrary"), vmem_limit_bytes=VMEM_LIMIT),
        name="xattn",
    )(x, gain, wq, wo, mk, mv)


def _prefix_matrix(c):
    nlev = int(math.log2(c))
    assert 1 << nlev == c
    tri = np.tril(np.ones((c, c), np.float32))
    return jnp.asarray(tri, dtype=BF16), nlev


def _upper_rows(x, b):
    return jnp.concatenate([x[s + b:s + 2 * b] for s in range(0, x.shape[0], 2 * b)], axis=0)


def _merge_rows(x, up, b):
    pieces = []
    for k, s in enumerate(range(0, x.shape[0], 2 * b)):
        pieces += [x[s:s + b], up[k * b:(k + 1) * b]]
    return jnp.concatenate(pieces, axis=0)


def _load_upper(ref, b):
    return jnp.concatenate([ref[s + b:s + 2 * b, :] for s in range(0, ref.shape[0], 2 * b)], axis=0)


def _store_upper(ref, up, b):
    for k, s in enumerate(range(0, ref.shape[0], 2 * b)):
        ref[s + b:s + 2 * b, :] = up[k * b:(k + 1) * b]


def _alternate(*gens):
    live = list(gens)
    while live:
        for g in list(live):
            if next(g, StopIteration) is StopIteration:
                live.remove(g)


def _mix_body(x_ref, g_ref, win_ref, wst_ref, cw_ref, vec_ref, rowp_ref, gup_ref, gb_ref, gng_ref,
              gnl_ref, wout_ref, cs_ref, conv0_ref, sg0_ref, sl0_ref,
              y_ref, convo_ref, sgo_ref, slo_ref,
              cbuf, sgdn, sgla, a_s, tinv_s, pm_s, *, tt, nlev, rows):
    t = pl.program_id(1)
    nt = pl.num_programs(1)
    tail = SUBLANES - (CONV_W - 1)

    @pl.when(t == 0)
    def _():
        cbuf[...] = jnp.zeros(cbuf.shape, F32)
        cbuf[:, tail:SUBLANES, :] = conv0_ref[...]
        sgdn[...] = sg0_ref[...]
        sgla[...] = sl0_ref[...]

    ii = lax.broadcasted_iota(jnp.int32, (tt, tt), 0)
    jj = lax.broadcasted_iota(jnp.int32, (tt, tt), 1)
    lower = ii > jj
    lv = jnp.where(lower, 31 - lax.clz(ii ^ jj), -1)
    eye = jnp.where(ii == jj, 1.0, 0.0).astype(F32)
    ii4 = jnp.concatenate([ii] * GLA_HEADS, axis=0)
    jj4 = jnp.concatenate([jj] * GLA_HEADS, axis=0)
    lv4 = jnp.concatenate([lv] * GLA_HEADS, axis=0)
    row_id = lax.broadcasted_iota(jnp.int32, (tt, GLA_QK), 0)
    lane_head = lax.broadcasted_iota(jnp.int32, (1, GLA_QK), 1) // GLA_DK
    head_mask = [jnp.where(lane_head == hd, 1.0, 0.0).astype(BF16) for hd in range(GLA_HEADS)]
    nq = GDN_QK // LANES

    def stack_heads(v):
        vb = v.astype(BF16)
        return jnp.concatenate([vb * m for m in head_mask], axis=0)

    def project(r, c):
        x = x_ref[r]
        h = _rms(x, g_ref[...]).astype(BF16)
        proj = lambda c0, c1: _dot(h, win_ref[:, c0:c1])
        c['x'] = x
        c['sm'] = proj(C_SM, IN_COLS_PAD)
        c['st'] = _dot_nt(wst_ref[...], h)
        yield
        c['p_qk2'] = proj(C_GQ, C_GV)
        yield
        c['proj'] = proj

    def front(r, c):
        sm, st, p_qk2 = c['sm'], c['st'], c['p_qk2']

        cw = cw_ref[...]
        blocks, tails = [], []
        for cc in range(0, GDN_CONV_CH, LANES):
            cs_ = slice(cc, cc + LANES)
            if cc % (2 * LANES) == 0:
                pair = c['proj'](C_QKV + cc, C_QKV + cc + 2 * LANES)
            seg = pair[:, cc % (2 * LANES):cc % (2 * LANES) + LANES]
            ext = jnp.concatenate([cbuf[r, :, cs_], seg], axis=0)
            conv = ext[tail:tail + tt] * cw[0:1, cs_]
            for i in range(1, CONV_W):
                conv = conv + ext[tail + i:tail + i + tt] * cw[i:i + 1, cs_]
            blocks.append(_silu(conv))
            tails.append(seg[tt - SUBLANES:tt])
            if len(blocks) % 3 == 0:
                yield
        last = jnp.concatenate(tails, axis=1)
        cbuf[r] = last
        c['new_tail'] = last[tail:SUBLANES]
        c['v2'] = c['proj'](C_GV, C_GR).astype(BF16)

        lfull = cs_ref[...]
        vec = vec_ref[...]
        beta_c = jax.nn.sigmoid(sm)
        g_c = vec[0:1, :] * _softplus(sm + vec[1:2, :])
        g_hi, g_lo = _split2(g_c)
        gcum_c = _dot(lfull, g_hi) + _dot(lfull, g_lo)
        rowp = rowp_ref[...]
        g_r = rowp[0] * _softplus(st + rowp[1])
        r_hi, r_lo = _split2(g_r)
        gcum_r = _dot_nt(r_hi, lfull) + _dot_nt(r_lo, lfull)

        c['gate_z'] = _silu(c['proj'](C_Z, C_GQ))
        yield

        xg = _dot(sm.astype(BF16), gup_ref[...]) + gb_ref[...]
        c['la'] = -_softplus(-xg) * (1.0 / GLA_TAU)
        c['q2'] = p_qk2[:, :GLA_QK] * (GLA_DK ** -0.5)
        c['k2'] = p_qk2[:, GLA_QK:]
        c['gate_r'] = _silu(c['proj'](C_GR, C_SM))
        yield

        hs = []
        for hd in range(GDN_HEADS):
            qh, kh, vh = blocks[hd], blocks[nq + hd], blocks[2 * nq + hd]
            qh = qh * lax.rsqrt(jnp.sum(qh * qh, axis=-1, keepdims=True) + EPS) * (GDN_DK ** -0.5)
            kh = kh * lax.rsqrt(jnp.sum(kh * kh, axis=-1, keepdims=True) + EPS)
            bcol = beta_c[:, SM_B + hd:SM_B + hd + 1]
            gc = gcum_c[:, SM_A + hd:SM_A + hd + 1]
            gr = gcum_r[SM_A + hd:SM_A + hd + 1, :]
            dec = jnp.exp(jnp.minimum(gc - gr, 0.0))
            kb = kh * bcol
            kk = _dot_nt(jnp.concatenate([kb, qh], axis=0).astype(BF16), kh.astype(BF16))
            a = jnp.where(lower, kk[:tt] * dec, 0.0)
            qk = jnp.where(ii >= jj, kk[tt:] * dec, 0.0).astype(BF16)
            eg = jnp.exp(gc)
            glast = gr[:, tt - 1:tt]
            a_s[r, hd] = a
            tinv_s[r, hd] = eye - jnp.where(lv == 0, a, 0.0)
            hs.append(dict(
                qk=qk,
                rhs=jnp.concatenate([vh * bcol, kb * eg], axis=1).astype(BF16),
                qg=(qh * eg).astype(BF16), kg=(kh * jnp.exp(glast - gc)).astype(BF16),
                eglast=jnp.exp(glast)))
            yield
        c['hs'] = hs

    def back(r, c):
        hs, q2, k2, v2, la = c['hs'], c['q2'], c['k2'], c['v2'], c['la']
        pm = pm_s.at[r]
        pm[...] = jnp.where(ii4 == jj4, _dot_nt(stack_heads(q2), k2.astype(BF16)), 0.0)
        rq, tot = la, la
        for l in range(nlev):
            b = 1 << l
            sliced = b % SUBLANES == 0
            if l >= 1:
                ys, tbs = [], []
                for hd in range(GDN_HEADS):
                    tb = tinv_s[r, hd].astype(BF16)
                    tbs.append(tb)
                    if sliced:
                        am = jnp.where(_upper_rows(lv, b) == l, _load_upper(a_s.at[r, hd], b), 0.0)
                        y = _dot(am.astype(BF16), tb).astype(BF16)
                        ys.append(_merge_rows(jnp.zeros((tt, tt), BF16), y, b))
                    else:
                        am = jnp.where(lv == l, a_s[r, hd], 0.0)
                        ys.append(_dot(am.astype(BF16), tb).astype(BF16))
                for hd in range(GDN_HEADS):
                    if sliced:
                        tu = _load_upper(tinv_s.at[r, hd], b)
                        _store_upper(tinv_s.at[r, hd], tu - _dot(tu.astype(BF16), ys[hd]), b)
                    else:
                        tinv_s[r, hd] = tinv_s[r, hd] - _dot(tbs[hd], ys[hd])
            qt = q2 * jnp.exp(rq)
            kt = (k2 * jnp.exp(tot - rq)).astype(BF16)
            if sliced:
                rr = _dot_nt(stack_heads(_upper_rows(qt, b)), kt)
                _store_upper(pm, jnp.where(_upper_rows(lv4, b) == l, rr, _load_upper(pm, b)), b)
            else:
                pm[...] = jnp.where(lv4 == l, _dot_nt(stack_heads(qt), kt), pm[...])
            prev = pltpu.roll(tot, b, axis=0)
            nxt = pltpu.roll(tot, tt - b, axis=0)
            hi_half = (row_id & b) != 0
            rq = rq + jnp.where(hi_half, prev, 0.0)
            tot = tot + jnp.where(hi_half, prev, nxt)
            yield
        bcum, brev = rq, tot - rq

        outs = []
        uws = [_dot(tinv_s[r, hd].astype(BF16), hs[hd]['rhs']) for hd in range(GDN_HEADS)]
        s_olds = [sgdn[r, hd] for hd in range(GDN_HEADS)]
        wqs = []
        zs = jnp.zeros((GDN_DK, GDN_DV), BF16)
        for ha in range(0, GDN_HEADS, 2):
            hb = ha + 1
            lhs = jnp.concatenate([
                jnp.concatenate([uws[ha][:, GDN_DV:], uws[hb][:, GDN_DV:]], axis=1).astype(BF16),
                jnp.concatenate([hs[ha]['qg'], hs[hb]['qg']], axis=1)], axis=0)
            sbd = jnp.concatenate([
                jnp.concatenate([s_olds[ha].astype(BF16), zs], axis=1),
                jnp.concatenate([zs, s_olds[hb].astype(BF16)], axis=1)], axis=0)
            res = _dot(lhs, sbd)
            wqs += [res[:, :GDN_DV], res[:, GDN_DV:]]
        unews = [(uw[:, :GDN_DV] - wq[:tt]).astype(BF16) for uw, wq in zip(uws, wqs)]
        yield
        for hd in range(GDN_HEADS):
            d = hs[hd]
            o = wqs[hd][tt:] + _dot(d['qk'], unews[hd])
            sgdn[r, hd] = s_olds[hd] * d['eglast'] + _dot_tn(d['kg'], unews[hd])
            on = _rms(o, gng_ref[...])
            outs.append(on * c['gate_z'][:, hd * GDN_DV:(hd + 1) * GDN_DV])
        yield

        s2 = sgla[r]
        row_head = lax.broadcasted_iota(jnp.int32, (GLA_QK, 1), 0) // GLA_DK
        s_bd = jnp.concatenate([jnp.where(row_head == hd, s2, 0.0) for hd in range(GLA_HEADS)],
                               axis=1).astype(BF16)
        inter = _dot((q2 * jnp.exp(bcum)).astype(BF16), s_bd)
        ds = _dot_tn((k2 * jnp.exp(brev)).astype(BF16), v2)
        la_hi, la_lo = _split2(la)
        ones = jnp.ones((tt, GLA_DV), BF16)
        blast = _dot_tn(la_hi, ones) + _dot_tn(la_lo, ones)
        yield
        for hd in range(GLA_HEADS):
            vs = slice(hd * GLA_DV, (hd + 1) * GLA_DV)
            rs = slice(hd * GLA_DK, (hd + 1) * GLA_DK)
            o = _dot(pm[hd * tt:(hd + 1) * tt, :].astype(BF16), v2[:, vs]) + inter[:, vs]
            sgla[r, rs, :] = s2[rs, :] * jnp.exp(blast[rs, :]) + ds[rs, vs]
            on = _rms(o, gnl_ref[...])
            outs.append(on * c['gate_r'][:, hd * GLA_DV:(hd + 1) * GLA_DV])
        yield

        o_all = jnp.concatenate(outs, axis=1).astype(BF16)
        y_ref[r] = c['x'] + _dot(o_all, wout_ref[...])
        yield

    ctx = [dict() for _ in range(rows)]
    stages = [g(r, ctx[r]) for r in range(rows) for g in (project, front, back)]
    order = [[stages[0]]]
    for r in range(rows):
        if r + 1 < rows:
            order.append([stages[3 * r + 1], stages[3 * r + 3]])
        else:
            order.append([stages[3 * r + 1]])
        if r >= 1:
            order[-1].insert(0, stages[3 * r - 1])
    order.append([stages[3 * rows - 1]])
    for group in order:
        if len(group) == 1:
            for _ in group[0]:
                pass
        else:
            _alternate(*group)

    @pl.when(t == nt - 1)
    def _():
        for r in range(rows):
            convo_ref[r] = ctx[r]['new_tail']
        sgo_ref[...] = sgdn[...]
        slo_ref[...] = sgla[...]


def _mix(x, l, W, conv0, sg0, sl0):
    b, t, _ = x.shape
    tt = min(t, MIX_TILE)
    nt = t // tt
    rows = MIX_ROWS if b % MIX_ROWS == 0 else 1
    cs, nlev = _prefix_matrix(tt)
    rowp = jnp.broadcast_to(W['rowp'][l][:, :, None], (2, 2 * SUBLANES, tt))
    wspec = lambda shp: pl.BlockSpec((None,) + shp, lambda i, j: (l,) + (0,) * len(shp))
    full = lambda shp: pl.BlockSpec(shp, lambda i, j: (0,) * len(shp))
    sspec = lambda shp: pl.BlockSpec((rows,) + shp, lambda i, j: (i,) + (0,) * len(shp))
    return pl.pallas_call(
        functools.partial(_mix_body, tt=tt, nlev=nlev, rows=rows),
        grid=(b // rows, nt),
        in_specs=[
            pl.BlockSpec((rows, tt, D_MODEL), lambda i, j: (i, j, 0)),
            wspec((1, D_MODEL)),
            wspec((D_MODEL, IN_COLS_PAD)),
            wspec((2 * SUBLANES, D_MODEL)),
            wspec((CONV_W, GDN_CONV_CH)),
            wspec((SUBLANES, LANES)),
            full((2, 2 * SUBLANES, tt)),
            wspec((LANES, GLA_QK)),
            wspec((1, GLA_QK)),
            wspec((1, GDN_DV)),
            wspec((1, GLA_DV)),
            wspec((MIX_W, D_MODEL)),
            full((tt, tt)),
            sspec((CONV_W - 1, GDN_CONV_CH)),
            sspec((GDN_HEADS, GDN_DK, GDN_DV)),
            sspec((GLA_QK, GLA_DV)),
        ],
        out_specs=[
            pl.BlockSpec((rows, tt, D_MODEL), lambda i, j: (i, j, 0)),
            sspec((CONV_W - 1, GDN_CONV_CH)),
            sspec((GDN_HEADS, GDN_DK, GDN_DV)),
            sspec((GLA_QK, GLA_DV)),
        ],
        out_shape=[
            jax.ShapeDtypeStruct(x.shape, F32),
            jax.ShapeDtypeStruct((b, CONV_W - 1, GDN_CONV_CH), F32),
            jax.ShapeDtypeStruct((b, GDN_HEADS, GDN_DK, GDN_DV), F32),
            jax.ShapeDtypeStruct((b, GLA_QK, GLA_DV), F32),
        ],
        scratch_shapes=[
            pltpu.VMEM((rows, SUBLANES, GDN_CONV_CH), F32),
            pltpu.VMEM((rows, GDN_HEADS, GDN_DK, GDN_DV), F32),
            pltpu.VMEM((rows, GLA_QK, GLA_DV), F32),
            pltpu.VMEM((rows, GDN_HEADS, tt, tt), F32),
            pltpu.VMEM((rows, GDN_HEADS, tt, tt), F32),
            pltpu.VMEM((rows, GLA_HEADS * tt, tt), F32),
        ],
        compiler_params=pltpu.CompilerParams(
            dimension_semantics=("arbitrary", "arbitrary"), vmem_limit_bytes=VMEM_LIMIT),
        name="mix",
    )(x, W['mix_norm'], W['w_in'], W['w_st'], W['gdn_conv_w'], W['vec'], rowp, W['gup'], W['gla_gate_bias'],
      W['gdn_out_norm'], W['gla_out_norm'], W['w_out'], cs, conv0, sg0, sl0)


def _prep_weights(ffn1_norm, ffn1_w_gate, ffn1_w_up, ffn1_w_down, mix_norm, w_in, gdn_conv_w, gdn_a_log,
                  gdn_dt_bias, gdn_out_norm, gla_gate_up, gla_gate_bias, gla_out_norm, w_out, xattn_norm,
                  xattn_w_q, xattn_w_o, ffn2_norm, ffn2_w_gate, ffn2_w_up, ffn2_w_down, final_norm):
    o = IN_OFFSETS
    seg = lambda k: w_in[:, :, o[k]:o[k] + IN_SIZES[k]]
    small = jnp.concatenate([seg(1), seg(2), seg(8)], axis=-1)
    small = jnp.pad(small, ((0, 0), (0, 0), (0, LANES - small.shape[-1])))
    w_in_r = jnp.concatenate([seg(0), seg(3), seg(4), seg(5), seg(6), seg(7), small], axis=-1).astype(BF16)
    w_st = jnp.concatenate([seg(1), seg(2)], axis=-1)
    w_st = jnp.pad(jnp.swapaxes(w_st, 1, 2), ((0, 0), (0, SUBLANES), (0, 0))).astype(BF16)
    nega = -jnp.exp(gdn_a_log.astype(F32))
    dtb = gdn_dt_bias.astype(F32)
    lane_pad = lambda v: jnp.pad(v, ((0, 0), (SM_A, LANES - SM_A - GDN_HEADS)))
    vec = jnp.stack([lane_pad(nega), lane_pad(dtb)], axis=1)
    vec = jnp.pad(vec, ((0, 0), (0, SUBLANES - 2), (0, 0)))
    row_pad = lambda v: jnp.pad(v, ((0, 0), (SM_A, 2 * SUBLANES - SM_A - GDN_HEADS)))
    rowp = jnp.stack([row_pad(nega), row_pad(dtb)], axis=1)
    gup = jnp.pad(gla_gate_up, ((0, 0), (SM_R, LANES - SM_R - GLA_RANK), (0, 0))).astype(BF16)
    r3 = lambda v: v[:, None, :].astype(F32)
    return dict(
        ffn1_norm=r3(ffn1_norm), ffn1_w_gate=ffn1_w_gate.astype(BF16), ffn1_w_up=ffn1_w_up.astype(BF16),
        ffn1_w_down=ffn1_w_down.astype(BF16), mix_norm=r3(mix_norm), w_in=w_in_r, w_st=w_st,
        gdn_conv_w=gdn_conv_w.astype(F32), vec=vec, rowp=rowp, gup=gup, gla_gate_bias=r3(gla_gate_bias),
        gdn_out_norm=r3(gdn_out_norm), gla_out_norm=r3(gla_out_norm), w_out=w_out.astype(BF16),
        xattn_norm=r3(xattn_norm), xattn_w_q=xattn_w_q.astype(BF16), xattn_w_o=xattn_w_o.astype(BF16),
        ffn2_norm=r3(ffn2_norm), ffn2_w_gate=ffn2_w_gate.astype(BF16), ffn2_w_up=ffn2_w_up.astype(BF16),
        ffn2_w_down=ffn2_w_down.astype(BF16), final_norm=final_norm[None, :].astype(F32))


def _trunk(x, mem_k, mem_v, conv_state, gdn_state, gla_state, W):
    b, t, _ = x.shape
    mk = mem_k.reshape(DEPTH, b, N_MEM, D_MODEL)
    mv = mem_v.reshape(DEPTH, b, N_MEM, D_MODEL)
    gla_state = gla_state.reshape(DEPTH, b, GLA_QK, GLA_DV)
    convs, gdns, glas = [], [], []
    for l in range(DEPTH):
        x2 = _ffn(x.reshape(b * t, D_MODEL), l, W['ffn1_norm'], W['ffn1_w_gate'], W['ffn1_w_up'],
                  W['ffn1_w_down'], W['final_norm'], False)
        x, cb, sa, sb = _mix(x2.reshape(b, t, D_MODEL), l, W, conv_state[l], gdn_state[l], gla_state[l])
        x = _xattn(x, l, W['xattn_norm'], W['xattn_w_q'], W['xattn_w_o'], mk, mv)
        x2 = _ffn(x.reshape(b * t, D_MODEL), l, W['ffn2_norm'], W['ffn2_w_gate'], W['ffn2_w_up'],
                  W['ffn2_w_down'], W['final_norm'], l == DEPTH - 1)
        x = x2.reshape(b, t, D_MODEL)
        convs.append(cb)
        gdns.append(sa)
        glas.append(sb.reshape(b, GLA_HEADS, GLA_DK, GLA_DV))
    return x, jnp.stack(convs), jnp.stack(gdns), jnp.stack(glas)


def kernel(x_prompt, x_sample, mem_prompt, cache_mem_k, cache_mem_v, state_gdn_conv, state_gdn, state_gla, ffn1_norm, ffn1_w_gate, ffn1_w_up, ffn1_w_down, mix_norm, w_in, gdn_conv_w, gdn_a_log, gdn_dt_bias, gdn_out_norm, gla_gate_up, gla_gate_bias, gla_out_norm, w_out, xattn_norm, xattn_w_q, xattn_w_k, xattn_w_v, xattn_w_o, ffn2_norm, ffn2_w_gate, ffn2_w_up, ffn2_w_down, final_norm):
    W = _prep_weights(ffn1_norm, ffn1_w_gate, ffn1_w_up, ffn1_w_down, mix_norm, w_in, gdn_conv_w, gdn_a_log,
                      gdn_dt_bias, gdn_out_norm, gla_gate_up, gla_gate_bias, gla_out_norm, w_out, xattn_norm,
                      xattn_w_q, xattn_w_o, ffn2_norm, ffn2_w_gate, ffn2_w_up, ffn2_w_down, final_norm)
    bp = x_prompt.shape[0]
    mem2 = mem_prompt.reshape(bp * N_MEM, D_MODEL).astype(BF16)
    mk_p, mv_p = _memkv(mem2, xattn_w_k.astype(BF16), xattn_w_v.astype(BF16))
    mem_k_p = mk_p.reshape(DEPTH, bp, N_MEM, XA_HEADS, XA_DH)
    mem_v_p = mv_p.reshape(DEPTH, bp, N_MEM, XA_HEADS, XA_DH)
    conv0 = jnp.zeros((DEPTH, bp, CONV_W - 1, GDN_CONV_CH), F32)
    gdn0 = jnp.zeros((DEPTH, bp, GDN_HEADS, GDN_DK, GDN_DV), F32)
    gla0 = jnp.zeros((DEPTH, bp, GLA_HEADS, GLA_DK, GLA_DV), F32)
    y_p, conv_p, gdn_p, gla_p = _trunk(x_prompt, mem_k_p, mem_v_p, conv0, gdn0, gla0, W)
    y_s, conv_s, gdn_s, gla_s = _trunk(x_sample, cache_mem_k, cache_mem_v, state_gdn_conv, state_gdn,
                                       state_gla, W)
    return (y_p, y_s, conv_p, gdn_p, gla_p, mem_k_p, mem_v_p, conv_s, gdn_s, gla_s)
```
